```python
import math
import jax, jax.numpy as jnp
from jax import lax
import numpy as np

D_MODEL = 1024
BATCH = 4
SEQ = 4096
DEPTH = 4

N_ATT_LAYERS = (DEPTH + 1) // 2
N_REC_LAYERS = DEPTH // 2
EPS = 1e-6

DA_HEADS = 8
DA_HEAD_DIM = 64
DA_PATTERNS = ((128, 1), (512, 4), (2048, 16))
DA_BLOCK = 128

MLA_HEADS = 8
MLA_Q_RANK = 256
MLA_KV_RANK = 128
MLA_NOPE = 64
MLA_ROPE = 32
MLA_V = 64
MLA_QK = MLA_NOPE + MLA_ROPE
ROPE_THETA = 10000.0
Q_BLOCK = 128

A_COLS = 3 * DA_HEADS * DA_HEAD_DIM
B_COLS = MLA_Q_RANK + MLA_KV_RANK + MLA_ROPE
MIX_IN = A_COLS + B_COLS
MIX_OUT = DA_HEADS * DA_HEAD_DIM + MLA_HEADS * MLA_V

LRU_WIDTH = D_MODEL
LRU_BLOCKS = 8
LRU_BLOCK_DIM = LRU_WIDTH // LRU_BLOCKS
CONV_WIDTH = 4
LRU_C = 8.0

N_EXPERTS = 32
TOP_K = 4
D_EXPERT = D_MODEL
SWIGLU_LIMIT = 7.0
SWIGLU_ALPHA = 1.702
MOE_BLOCK = 256

kernel_name = "hybrid_dilated_mla_rglru_moe_adaln"


def rms_norm(x, gain):
    xf = x.astype(jnp.float32)
    y = xf * lax.rsqrt(jnp.mean(xf * xf, axis=-1, keepdims=True) + EPS)
    return (y * gain.astype(jnp.float32)).astype(x.dtype)


def modulate(h, shift, scale):
    return h * (1 + scale[:, None, :]) + shift[:, None, :]


def rope_tables(positions):
    inv = ROPE_THETA ** (-jnp.arange(0, MLA_ROPE, 2, dtype=jnp.float32) / MLA_ROPE)
    ang = positions.astype(jnp.float32)[..., None] * inv
    return jnp.cos(ang), jnp.sin(ang)


def apply_rope(x, cos, sin):
    x1, x2 = jnp.split(x.astype(jnp.float32), 2, axis=-1)
    return jnp.concatenate([x1 * cos - x2 * sin, x1 * sin + x2 * cos], axis=-1).astype(x.dtype)


def dilated_window_attention(q, k, v, window, dilation):
    B, S, H, E = q.shape
    n_back = window // dilation
    L = S // dilation
    Lp = -(-L // DA_BLOCK) * DA_BLOCK
    nb = Lp // DA_BLOCK

    def to_sub(t):
        t = t.reshape(B, L, dilation, H, E).transpose(0, 2, 3, 1, 4)
        t = jnp.pad(t, ((0, 0), (0, 0), (0, 0), (0, Lp - L), (0, 0)))
        return t.reshape(B, dilation, H, nb, DA_BLOCK, E)

    def with_prev(t):
        prev = jnp.pad(t, ((0, 0), (0, 0), (0, 0), (1, 0), (0, 0), (0, 0)))[:, :, :, :-1]
        return jnp.concatenate([prev, t], axis=4)

    qs = to_sub(q)
    kk = with_prev(to_sub(k))
    vv = with_prev(to_sub(v))
    s = jnp.einsum('brhnqe,brhnke->brhnqk', qs, kk,
                   preferred_element_type=jnp.float32) * (E ** -0.5)
    dist = (jnp.arange(DA_BLOCK)[:, None] + DA_BLOCK) - jnp.arange(2 * DA_BLOCK)[None, :]
    key_sub = (jnp.arange(nb)[:, None, None] - 1) * DA_BLOCK + jnp.arange(2 * DA_BLOCK)[None, None, :]
    mask = (dist >= 0) & (dist <= n_back) & (key_sub >= 0)
    s = jnp.where(mask, s, -jnp.inf)
    m = jnp.max(s, axis=-1, keepdims=True)
    p = jnp.exp(s - m)
    l = jnp.sum(p, axis=-1, keepdims=True)
    o = jnp.einsum('brhnqk,brhnke->brhnqe', (p / l).astype(v.dtype), vv)
    lse = (m + jnp.log(l))[..., 0]

    def from_sub(t):
        rest = t.shape[5:]
        t = t.reshape((B, dilation, H, Lp) + rest)[:, :, :, :L]
        t = jnp.moveaxis(t, 3, 1)
        return t.reshape((B, S, H) + rest)

    return from_sub(o), from_sub(lse)


def causal_block_attention(q, k, v, scale):
    B, S, H, E = q.shape
    nq = S // Q_BLOCK
    qb = q.reshape(B, nq, Q_BLOCK, H, E).transpose(1, 0, 2, 3, 4)
    kpos = jnp.arange(S)

    def one_block(args):
        qi, i = args
        s = jnp.einsum('bqhe,bkhe->bhqk', qi, k, preferred_element_type=jnp.float32) * scale
        qpos = i * Q_BLOCK + jnp.arange(Q_BLOCK)
        s = jnp.where(kpos[None, :] <= qpos[:, None], s, -jnp.inf)
        p = jax.nn.softmax(s, axis=-1)
        return jnp.einsum('bhqk,bkhe->bqhe', p.astype(v.dtype), v)

    o = lax.map(one_block, (qb, jnp.arange(nq)))
    return o.transpose(1, 0, 2, 3, 4).reshape(B, S, H, v.shape[-1])


def attention_mixer(h, w_in, da_q_gain, da_k_gain, qa_gain, kva_gain, w_uq, w_ukv,
                    q_gain, k_gain, w_out, cos, sin):
    B, S, _ = h.shape
    proj = h @ w_in
    a_qkv, c_q, c_kv, k_rope = jnp.split(
        proj, [A_COLS, A_COLS + MLA_Q_RANK, A_COLS + MLA_Q_RANK + MLA_KV_RANK], axis=-1)

    a_qkv = a_qkv.reshape(B, S, 3, DA_HEADS, DA_HEAD_DIM)
    qa = rms_norm(a_qkv[:, :, 0], da_q_gain)
    ka = rms_norm(a_qkv[:, :, 1], da_k_gain)
    va = a_qkv[:, :, 2]
    outs, lses = [], []
    for window, dilation in DA_PATTERNS:
        o_i, lse_i = dilated_window_attention(qa, ka, va, window, dilation)
        outs.append(o_i)
        lses.append(lse_i)
    mix_w = jax.nn.softmax(jnp.stack(lses, axis=-1), axis=-1)
    o_a = jnp.einsum('bshp,pbshe->bshe', mix_w, jnp.stack(outs, axis=0).astype(jnp.float32))
    o_a = o_a.astype(h.dtype).reshape(B, S, DA_HEADS * DA_HEAD_DIM)

    q = (rms_norm(c_q, qa_gain) @ w_uq).reshape(B, S, MLA_HEADS, MLA_QK)
    kv = (rms_norm(c_kv, kva_gain) @ w_ukv).reshape(B, S, MLA_HEADS, MLA_NOPE + MLA_V)
    q_nope, q_pe = q[..., :MLA_NOPE], q[..., MLA_NOPE:]
    k_nope, v_b = kv[..., :MLA_NOPE], kv[..., MLA_NOPE:]
    q_nope = rms_norm(q_nope, q_gain[:MLA_NOPE])
    q_pe = apply_rope(rms_norm(q_pe, q_gain[MLA_NOPE:]), cos[:, :, None], sin[:, :, None])
    k_nope = rms_norm(k_nope, k_gain[:MLA_NOPE])
    k_pe = apply_rope(rms_norm(k_rope, k_gain[MLA_NOPE:]), cos, sin)
    qb = jnp.concatenate([q_nope, q_pe], axis=-1)
    kb = jnp.concatenate([k_nope, jnp.broadcast_to(k_pe[:, :, None], (B, S, MLA_HEADS, MLA_ROPE))], axis=-1)
    o_b = causal_block_attention(qb, kb, v_b, MLA_QK ** -0.5).reshape(B, S, MLA_HEADS * MLA_V)

    return jnp.concatenate([o_a, o_b], axis=-1) @ w_out


def rg_lru(x, w_a, b_a, w_x, b_x, lam):
    B, S, W = x.shape
    xb = x.reshape(B, S, LRU_BLOCKS, LRU_BLOCK_DIM)
    r = jax.nn.sigmoid(jnp.einsum('bsnd,nde->bsne', xb, w_a).reshape(B, S, W) + b_a)
    i = jax.nn.sigmoid(jnp.einsum('bsnd,nde->bsne', xb, w_x).reshape(B, S, W) + b_x)
    log_a = -LRU_C * r.astype(jnp.float32) * jax.nn.softplus(-lam.astype(jnp.float32))
    a = jnp.exp(log_a)
    mult = jnp.sqrt(-jnp.expm1(2.0 * log_a))
    bterm = mult * (i * x).astype(jnp.float32)

    def combine(left, right):
        a1, b1 = left
        a2, b2 = right
        return a1 * a2, a2 * b1 + b2

    _, hseq = lax.associative_scan(combine, (a, bterm), axis=1)
    return hseq.astype(x.dtype)


def recurrent_mixer(h, w_in, b_in, conv_w, conv_b, w_a, b_a, w_x, b_x, lam, w_out, b_out):
    gx = h @ w_in + b_in
    g, xr = gx[..., :LRU_WIDTH], gx[..., LRU_WIDTH:]
    g = jax.nn.gelu(g)
    xr = lax.conv_general_dilated(
        xr, conv_w[:, None, :], window_strides=(1,), padding=[(CONV_WIDTH - 1, 0)],
        dimension_numbers=('NWC', 'WIO', 'NWC'), feature_group_count=LRU_WIDTH) + conv_b
    y = rg_lru(xr, w_a, b_a, w_x, b_x, lam)
    return (g * y) @ w_out + b_out


def moe_ffn(h, w_router, b_router, w1, b1, w2, b2):
    B, S, D = h.shape
    T = B * S
    xt = h.reshape(T, D)
    logits = (xt @ w_router + b_router).astype(jnp.float32)
    top_vals, top_idx = lax.top_k(logits, TOP_K)
    gates = jax.nn.softmax(top_vals, axis=-1)

    M = T * TOP_K
    e_flat = top_idx.reshape(M)
    tok_flat = jnp.repeat(jnp.arange(T, dtype=jnp.int32), TOP_K)
    g_flat = gates.reshape(M)
    order = jnp.argsort(e_flat)
    e_sorted, tok_sorted, g_sorted = e_flat[order], tok_flat[order], g_flat[order]
    counts = jnp.bincount(e_flat, length=N_EXPERTS)
    starts = jnp.cumsum(counts) - counts
    padded = (counts + MOE_BLOCK - 1) // MOE_BLOCK * MOE_BLOCK
    pad_ends = jnp.cumsum(padded)
    pad_starts = pad_ends - padded
    dest = pad_starts[e_sorted] + (jnp.arange(M) - starts[e_sorted])
    n_blocks = M // MOE_BLOCK + N_EXPERTS
    P = n_blocks * MOE_BLOCK
    row_tok = jnp.zeros((P,), jnp.int32).at[dest].set(tok_sorted)
    row_gate = jnp.zeros((P,), jnp.float32).at[dest].set(g_sorted)
    block_expert = jnp.minimum(
        jnp.searchsorted(pad_ends, jnp.arange(n_blocks) * MOE_BLOCK, side='right'), N_EXPERTS - 1)

    def expert_block(args):
        toks, e = args
        hb = xt[toks] @ w1[e] + b1[e]
        gate = jnp.minimum(hb[:, ::2], SWIGLU_LIMIT)
        lin = jnp.clip(hb[:, 1::2], -SWIGLU_LIMIT, SWIGLU_LIMIT)
        act = gate * jax.nn.sigmoid(SWIGLU_ALPHA * gate) * (lin + 1)
        return act @ w2[e] + b2[e]

    rows = lax.map(expert_block, (row_tok.reshape(n_blocks, MOE_BLOCK), block_expert))
    rows = rows.reshape(P, D).astype(jnp.float32) * row_gate[:, None]
    y = jax.ops.segment_sum(rows, row_tok, num_segments=T)
    return y.reshape(B, S, D).astype(h.dtype)


def setup_inputs(seed: int = 0) -> dict:
    key = jax.random.key(seed)
    ks = iter(jax.random.split(key, 40))
    nrm = lambda shape, s: jax.random.normal(next(ks), shape, jnp.float32) * s
    gain = lambda shape: 1.0 + nrm(shape, 0.02)
    NA, NR, D = N_ATT_LAYERS, N_REC_LAYERS, D_MODEL

    x = nrm((BATCH, SEQ, D), 1.0)
    c = nrm((BATCH, D), 1.0)
    positions = (jnp.arange(SEQ, dtype=jnp.int32)[None, :]
                 + jax.random.randint(next(ks), (BATCH, 1), 0, 4096, dtype=jnp.int32))

    a0 = jax.random.uniform(next(ks), (NR, LRU_WIDTH), jnp.float32, 0.9, 0.999)
    p0 = a0 ** (1.0 / LRU_C)
    rec_lambda = jnp.log(p0) - jnp.log1p(-p0)

    return {
        "x": x,
        "c": c,
        "positions": positions,
        "ada_w": nrm((DEPTH, D, 6 * D), D ** -0.5),
        "ada_b": nrm((DEPTH, 6 * D), 0.02),
        "norm1_gain": gain((DEPTH, D)),
        "norm2_gain": gain((DEPTH, D)),
        "attn_w_in": nrm((NA, D, MIX_IN), D ** -0.5),
        "da_q_gain": gain((NA, DA_HEAD_DIM)),
        "da_k_gain": gain((NA, DA_HEAD_DIM)),
        "mla_qa_gain": gain((NA, MLA_Q_RANK)),
        "mla_kva_gain": gain((NA, MLA_KV_RANK)),
        "mla_w_uq": nrm((NA, MLA_Q_RANK, MLA_HEADS * MLA_QK), MLA_Q_RANK ** -0.5),
        "mla_w_ukv": nrm((NA, MLA_KV_RANK, MLA_HEADS * (MLA_NOPE + MLA_V)), MLA_KV_RANK ** -0.5),
        "mla_q_gain": gain((NA, MLA_QK)),
        "mla_k_gain": gain((NA, MLA_QK)),
        "attn_w_out": nrm((NA, MIX_OUT, D), MIX_OUT ** -0.5),
        "rec_w_in": nrm((NR, D, 2 * LRU_WIDTH), D ** -0.5),
        "rec_b_in": nrm((NR, 2 * LRU_WIDTH), 0.02),
        "rec_conv_w": nrm((NR, CONV_WIDTH, LRU_WIDTH), CONV_WIDTH ** -0.5),
        "rec_conv_b": nrm((NR, LRU_WIDTH), 0.02),
        "rec_wa": nrm((NR, LRU_BLOCKS, LRU_BLOCK_DIM, LRU_BLOCK_DIM), LRU_BLOCK_DIM ** -0.5),
        "rec_ba": nrm((NR, LRU_WIDTH), 0.02),
        "rec_wx": nrm((NR, LRU_BLOCKS, LRU_BLOCK_DIM, LRU_BLOCK_DIM), LRU_BLOCK_DIM ** -0.5),
        "rec_bx": nrm((NR, LRU_WIDTH), 0.02),
        "rec_lambda": rec_lambda,
        "rec_w_out": nrm((NR, LRU_WIDTH, D), LRU_WIDTH ** -0.5),
        "rec_b_out": nrm((NR, D), 0.02),
        "moe_w_router": nrm((DEPTH, D, N_EXPERTS), D ** -0.5),
        "moe_b_router": nrm((DEPTH, N_EXPERTS), 0.01),
        "moe_w1": nrm((DEPTH, N_EXPERTS, D, 2 * D_EXPERT), D ** -0.5),
        "moe_b1": nrm((DEPTH, N_EXPERTS, 2 * D_EXPERT), 0.01),
        "moe_w2": nrm((DEPTH, N_EXPERTS, D_EXPERT, D), D_EXPERT ** -0.5),
        "moe_b2": nrm((DEPTH, N_EXPERTS, D), 0.01),
    }


def reference(x, c, positions, ada_w, ada_b, norm1_gain, norm2_gain,
              attn_w_in, da_q_gain, da_k_gain, mla_qa_gain, mla_kva_gain,
              mla_w_uq, mla_w_ukv, mla_q_gain, mla_k_gain, attn_w_out,
              rec_w_in, rec_b_in, rec_conv_w, rec_conv_b, rec_wa, rec_ba,
              rec_wx, rec_bx, rec_lambda, rec_w_out, rec_b_out,
              moe_w_router, moe_b_router, moe_w1, moe_b1, moe_w2, moe_b2):
    cos, sin = rope_tables(positions)
    c_act = jax.nn.silu(c)
    for layer in range(DEPTH):
        mod = c_act @ ada_w[layer] + ada_b[layer]
        sh1, sc1, g1, sh2, sc2, g2 = jnp.split(mod, 6, axis=-1)
        hn = modulate(rms_norm(x, norm1_gain[layer]), sh1, sc1)
        j = layer // 2
        if layer % 2 == 0:
            y = attention_mixer(hn, attn_w_in[j], da_q_gain[j], da_k_gain[j],
                                mla_qa_gain[j], mla_kva_gain[j], mla_w_uq[j], mla_w_ukv[j],
                                mla_q_gain[j], mla_k_gain[j], attn_w_out[j], cos, sin)
        else:
            y = recurrent_mixer(hn, rec_w_in[j], rec_b_in[j], rec_conv_w[j], rec_conv_b[j],
                                rec_wa[j], rec_ba[j], rec_wx[j], rec_bx[j], rec_lambda[j],
                                rec_w_out[j], rec_b_out[j])
        x = x + g1[:, None, :] * y
        hn = modulate(rms_norm(x, norm2_gain[layer]), sh2, sc2)
        x = x + g2[:, None, :] * moe_ffn(hn, moe_w_router[layer], moe_b_router[layer],
                                         moe_w1[layer], moe_b1[layer], moe_w2[layer], moe_b2[layer])
    return x
```

```python
import functools

import jax
import jax.numpy as jnp
from jax import lax
from jax.experimental import pallas as pl
from jax.experimental.pallas import tpu as pltpu

F32 = jnp.float32
BF16 = jnp.bfloat16
I32 = jnp.int32

D_MODEL = 1024
BATCH = 4
SEQ = 4096
TOKENS = BATCH * SEQ
DEPTH = 4
EPS = 1e-6

DA_HEADS = 8
DA_HEAD_DIM = 64
DA_PATTERNS = ((128, 1), (512, 4), (2048, 16))
DA_BLOCK = 128
DA_WIDTH = DA_HEADS * DA_HEAD_DIM

MLA_HEADS = 8
MLA_Q_RANK = 256
MLA_KV_RANK = 128
MLA_NOPE = 64
MLA_ROPE = 32
MLA_HALF = MLA_ROPE // 2
MLA_V = 64
MLA_QK = MLA_NOPE + MLA_ROPE
ROPE_THETA = 10000.0
A_COLS = 3 * DA_WIDTH

LRU_WIDTH = D_MODEL
LRU_BLOCKS = 8
LRU_BLOCK_DIM = LRU_WIDTH // LRU_BLOCKS
CONV_WIDTH = 4
LRU_C = 8.0

N_EXPERTS = 32
TOP_K = 4
D_EXPERT = D_MODEL
SWIGLU_LIMIT = 7.0
SWIGLU_ALPHA = 1.702

LANES = 128
SUBLANES = 8
VMEM_LIMIT = 56 * 1024 * 1024

ROW_TILE = 256
MOE_ROWS = 256
COMBINE_ROWS = 128
SCAN_ROWS = 128
MLA_TQ = 128
MLA_TK = 256

NT_DIMS = (((1,), (1,)), ((), ()))


def _params(sem, vmem=VMEM_LIMIT):
    return pltpu.CompilerParams(dimension_semantics=sem, vmem_limit_bytes=vmem)


def _prenorm(x, gain, shift, scale):
    ms = jnp.mean(x * x, axis=-1, keepdims=True)
    return (x * lax.rsqrt(ms + EPS) * gain) * (1.0 + scale) + shift


def _group_norm64(y):
    rows = y.shape[0]
    lo = lax.broadcasted_iota(I32, (rows, LANES), 1) < 64
    outs = []
    for j in range(y.shape[1] // LANES):
        c = y[:, j * LANES:(j + 1) * LANES]
        sq = c * c
        s0 = jnp.sum(jnp.where(lo, sq, 0.0), axis=-1, keepdims=True)
        s1 = jnp.sum(jnp.where(lo, 0.0, sq), axis=-1, keepdims=True)
        r = jnp.where(lo, lax.rsqrt(s0 * (1.0 / 64) + EPS), lax.rsqrt(s1 * (1.0 / 64) + EPS))
        outs.append(c * r)
    return jnp.concatenate(outs, axis=-1)


def _ada_body(c_ref, w_ref, b_ref, o_ref):
    c = c_ref[...]
    ca = c * jax.nn.sigmoid(c)
    o_ref[0] = jnp.dot(ca.astype(BF16), w_ref[0].astype(BF16), preferred_element_type=F32) + b_ref[0]


def _ada(c, ada_w, ada_b):
    c8 = jnp.pad(c, ((0, SUBLANES - BATCH), (0, 0)))
    out = pl.pallas_call(
        _ada_body,
        grid=(DEPTH, 6),
        in_specs=[pl.BlockSpec((SUBLANES, D_MODEL), lambda l, j: (0, 0)),
                  pl.BlockSpec((1, D_MODEL, D_MODEL), lambda l, j: (l, 0, j)),
                  pl.BlockSpec((1, 1, D_MODEL), lambda l, j: (l, 0, j))],
        out_specs=pl.BlockSpec((1, SUBLANES, D_MODEL), lambda l, j: (l, 0, j)),
        out_shape=jax.ShapeDtypeStruct((DEPTH, SUBLANES, 6 * D_MODEL), F32),
        compiler_params=_params(("arbitrary", "arbitrary")),
        name="ada",
    )(c8, ada_w, ada_b.reshape(DEPTH, 1, 6 * D_MODEL))
    return out[:, :BATCH].reshape(DEPTH, BATCH * 6, 1, D_MODEL)


def _mod_spec(j, rows_per_batch_tiles):
    return pl.BlockSpec((1, 1, D_MODEL), lambda i: ((i // rows_per_batch_tiles) * 6 + j, 0, 0))


def _rope_body(pos_ref, inv_ref, cos_ref, sin_ref):
    ang = pos_ref[...].astype(F32) * inv_ref[...]
    cos_ref[...] = jnp.cos(ang)
    sin_ref[...] = jnp.sin(ang)


def _rope_tables(positions):
    inv = ROPE_THETA ** (-jnp.arange(0, MLA_ROPE, 2, dtype=F32) / MLA_ROPE)
    inv_t = jnp.tile(inv, MLA_HEADS).reshape(1, LANES)
    tm = 512
    return pl.pallas_call(
        _rope_body,
        grid=(TOKENS // tm,),
        in_specs=[pl.BlockSpec((tm, 1), lambda i: (i, 0)),
                  pl.BlockSpec((1, LANES), lambda i: (0, 0))],
        out_specs=[pl.BlockSpec((tm, LANES), lambda i: (i, 0))] * 2,
        out_shape=[jax.ShapeDtypeStruct((TOKENS, LANES), F32)] * 2,
        compiler_params=_params(("arbitrary",)),
        name="rope_tables",
    )(positions.reshape(TOKENS, 1), inv_t)


IN_ATTN_COLS = A_COLS + MLA_Q_RANK + MLA_KV_RANK + 2 * LANES


def _in_attn_body(x_ref, sh_ref, sc_ref, g_ref, w_ref, gq_ref, gk_ref, gcq_ref, gckv_ref,
                  gk1_ref, gk2_ref, cos_ref, sin_ref,
                  qa_ref, ka_ref, va_ref, cq_ref, ckv_ref, kpe_ref):
    h = _prenorm(x_ref[...], g_ref[...], sh_ref[0], sc_ref[0])
    proj = jnp.dot(h.astype(BF16), w_ref[...], preferred_element_type=F32)
    w = DA_WIDTH
    qa_ref[...] = _group_norm64(proj[:, 0:w]) * gq_ref[...] * (DA_HEAD_DIM ** -0.5)
    ka_ref[...] = _group_norm64(proj[:, w:2 * w]) * gk_ref[...]
    va_ref[...] = proj[:, 2 * w:3 * w]
    o = A_COLS
    cq = proj[:, o:o + MLA_Q_RANK]
    cq_ref[...] = (cq * lax.rsqrt(jnp.mean(cq * cq, axis=-1, keepdims=True) + EPS) * gcq_ref[...]).astype(BF16)
    o += MLA_Q_RANK
    ckv = proj[:, o:o + MLA_KV_RANK]
    ckv_ref[...] = (ckv * lax.rsqrt(jnp.mean(ckv * ckv, axis=-1, keepdims=True) + EPS) * gckv_ref[...]).astype(BF16)
    o += MLA_KV_RANK
    x1 = proj[:, o:o + LANES]
    x2 = proj[:, o + LANES:o + 2 * LANES]
    ss = jnp.sum(x1 * x1 + x2 * x2, axis=-1, keepdims=True) * (1.0 / (MLA_HEADS * MLA_ROPE))
    rs = lax.rsqrt(ss + EPS)
    n1 = x1 * rs * gk1_ref[...]
    n2 = x2 * rs * gk2_ref[...]
    cos = cos_ref[...]
    sin = sin_ref[...]
    kpe_ref[:, 0:LANES] = (n1 * cos - n2 * sin).astype(BF16)
    kpe_ref[:, LANES:2 * LANES] = (n1 * sin + n2 * cos).astype(BF16)


def _in_attn(x, mods, gain, w_p, gq, gk, gcq, gckv, gk1, gk2, cos_t, sin_t):
    tm = ROW_TILE
    tpb = SEQ // tm
    row = lambda cols: pl.BlockSpec((tm, cols), lambda i: (i, 0))
    full = lambda shape: pl.BlockSpec(shape, lambda i: (0,) * len(shape))
    return pl.pallas_call(
        _in_attn_body,
        grid=(TOKENS // tm,),
        in_specs=[row(D_MODEL), _mod_spec(0, tpb), _mod_spec(1, tpb), full((1, D_MODEL)),
                  full((D_MODEL, IN_ATTN_COLS)), full((1, DA_WIDTH)), full((1, DA_WIDTH)),
                  full((1, MLA_Q_RANK)), full((1, MLA_KV_RANK)), full((1, LANES)), full((1, LANES)),
                  row(LANES), row(LANES)],
        out_specs=[row(DA_WIDTH), row(DA_WIDTH), row(DA_WIDTH), row(MLA_Q_RANK), row(MLA_KV_RANK),
                   row(2 * LANES)],
        out_shape=[jax.ShapeDtypeStruct((TOKENS, DA_WIDTH), F32)] * 3
        + [jax.ShapeDtypeStruct((TOKENS, MLA_Q_RANK), BF16),
           jax.ShapeDtypeStruct((TOKENS, MLA_KV_RANK), BF16),
           jax.ShapeDtypeStruct((TOKENS, 2 * LANES), BF16)],
        compiler_params=_params(("arbitrary",)),
        name="in_attn",
    )(x, mods, mods, gain, w_p, gq, gk, gcq, gckv, gk1, gk2, cos_t, sin_t)


MLA_Q_COLS = MLA_HEADS * MLA_NOPE + 2 * LANES


def _mla_up_body(cq_ref, ckv_ref, wq_ref, wkv_ref, gqn_ref, gq1_ref, gq2_ref, gkn_ref, cos_ref, sin_ref,
                 qb_ref, kn_ref, vb_ref):
    q = jnp.dot(cq_ref[...], wq_ref[...], preferred_element_type=F32)
    kv = jnp.dot(ckv_ref[...], wkv_ref[...], preferred_element_type=F32)
    nw = MLA_HEADS * MLA_NOPE
    scale = MLA_QK ** -0.5
    qb_ref[:, 0:nw] = (_group_norm64(q[:, 0:nw]) * gqn_ref[...] * scale).astype(BF16)
    x1 = q[:, nw:nw + LANES]
    x2 = q[:, nw + LANES:nw + 2 * LANES]
    ssq = x1 * x1 + x2 * x2
    gi = lax.broadcasted_iota(I32, (LANES, LANES), 0) // MLA_HALF
    gj = lax.broadcasted_iota(I32, (LANES, LANES), 1) // MLA_HALF
    gmat = jnp.where(gi == gj, 1.0, 0.0).astype(BF16)
    hi = ssq.astype(BF16)
    lo = (ssq - hi.astype(F32)).astype(BF16)
    ss = (jnp.dot(hi, gmat, preferred_element_type=F32) + jnp.dot(lo, gmat, preferred_element_type=F32))
    rs = lax.rsqrt(ss * (1.0 / MLA_ROPE) + EPS)
    n1 = x1 * rs * gq1_ref[...]
    n2 = x2 * rs * gq2_ref[...]
    cos = cos_ref[...]
    sin = sin_ref[...]
    qb_ref[:, nw:nw + LANES] = ((n1 * cos - n2 * sin) * scale).astype(BF16)
    qb_ref[:, nw + LANES:nw + 2 * LANES] = ((n1 * sin + n2 * cos) * scale).astype(BF16)
    kn_ref[...] = (_group_norm64(kv[:, 0:nw]) * gkn_ref[...]).astype(BF16)
    vb_ref[...] = kv[:, nw:].astype(BF16)


def _mla_up(cqn, ckvn, wq_p, wkv_p, gqn, gq1, gq2, gkn, cos_t, sin_t):
    tm = ROW_TILE
    nw = MLA_HEADS * MLA_NOPE
    row = lambda cols: pl.BlockSpec((tm, cols), lambda i: (i, 0))
    full = lambda shape: pl.BlockSpec(shape, lambda i: (0,) * len(shape))
    return pl.pallas_call(
        _mla_up_body,
        grid=(TOKENS // tm,),
        in_specs=[row(MLA_Q_RANK), row(MLA_KV_RANK), full((MLA_Q_RANK, MLA_Q_COLS)),
                  full((MLA_KV_RANK, 2 * nw)), full((1, nw)), full((1, LANES)), full((1, LANES)),
                  full((1, nw)), row(LANES), row(LANES)],
        out_specs=[row(MLA_Q_COLS), row(nw), row(nw)],
        out_shape=[jax.ShapeDtypeStruct((TOKENS, MLA_Q_COLS), BF16),
                   jax.ShapeDtypeStruct((TOKENS, nw), BF16),
                   jax.ShapeDtypeStruct((TOKENS, nw), BF16)],
        compiler_params=_params(("arbitrary",)),
        name="mla_up",
    )(cqn, ckvn, wq_p, wkv_p, gqn, gq1, gq2, gkn, cos_t, sin_t)


def _mla_attn_body(qn_ref, qr1_ref, qr2_ref, kn_ref, kpe_ref, v_ref, o_ref):
    tq, tk = MLA_TQ, MLA_TK
    p = pl.program_id(1)
    qi = pl.program_id(2)
    lane = lax.broadcasted_iota(I32, (tq, LANES), 1)
    lo = lane < 64
    qn = qn_ref[...]
    r1 = qr1_ref[...]
    r2 = qr2_ref[...]
    zero = jnp.zeros_like(qn)
    head = lane // MLA_HALF
    pe0 = head == 2 * p
    pe1 = head == 2 * p + 1
    qnn = jnp.concatenate([jnp.where(lo, qn, zero), jnp.where(lo, zero, qn)], axis=0)
    qpe = jnp.concatenate(
        [jnp.concatenate([jnp.where(pe0, r1, zero), jnp.where(pe0, r2, zero)], axis=1),
         jnp.concatenate([jnp.where(pe1, r1, zero), jnp.where(pe1, r2, zero)], axis=1)], axis=0)

    def step(ki, carry, masked):
        m, l, acc = carry
        ks = pl.multiple_of(ki * tk, tk)
        kn = kn_ref[pl.ds(ks, tk), :]
        kp = kpe_ref[pl.ds(ks, tk), :]
        v = v_ref[pl.ds(ks, tk), :]
        s = (lax.dot_general(qnn, kn, NT_DIMS, preferred_element_type=F32)
             + lax.dot_general(qpe, kp, NT_DIMS, preferred_element_type=F32))
        if masked:
            rowg = qi * tq + lax.broadcasted_iota(I32, (2 * tq, tk), 0) % tq
            colg = ki * tk + lax.broadcasted_iota(I32, (2 * tq, tk), 1)
            s = jnp.where(colg <= rowg, s, -jnp.inf)
        m_new = jnp.maximum(m, jnp.max(s, axis=-1, keepdims=True))
        alpha = jnp.exp(m - m_new)
        pexp = jnp.exp(s - m_new)
        l_new = alpha * l + jnp.sum(pexp, axis=-1, keepdims=True)
        acc_new = alpha * acc + jnp.dot(pexp.astype(BF16), v, preferred_element_type=F32)
        return m_new, l_new, acc_new

    init = (jnp.full((2 * tq, 1), -jnp.inf, F32), jnp.zeros((2 * tq, 1), F32),
            jnp.zeros((2 * tq, LANES), F32))
    kd = (qi * tq) // tk
    carry = lax.fori_loop(0, kd, lambda ki, c: step(ki, c, False), init)
    _, l, acc = step(kd, carry, True)
    out = acc / l
    o_ref[...] = jnp.where(lo, out[:tq], out[tq:]).astype(BF16)


def _mla_attn(qb, kn, kpe, vb):
    tq = MLA_TQ
    nq = SEQ // tq
    pairs = MLA_HEADS // 2
    return pl.pallas_call(
        _mla_attn_body,
        grid=(BATCH, pairs, nq),
        in_specs=[pl.BlockSpec((tq, LANES), lambda b, p, q: (b * nq + q, p)),
                  pl.BlockSpec((tq, LANES), lambda b, p, q: (b * nq + q, pairs)),
                  pl.BlockSpec((tq, LANES), lambda b, p, q: (b * nq + q, pairs + 1)),
                  pl.BlockSpec((SEQ, LANES), lambda b, p, q: (b, p)),
                  pl.BlockSpec((SEQ, 2 * LANES), lambda b, p, q: (b, 0)),
                  pl.BlockSpec((SEQ, LANES), lambda b, p, q: (b, p))],
        out_specs=pl.BlockSpec((tq, LANES), lambda b, p, q: (b * nq + q, p)),
        out_shape=jax.ShapeDtypeStruct((TOKENS, MLA_HEADS * MLA_V), BF16),
        compiler_params=_params(("arbitrary", "arbitrary", "arbitrary")),
        name="mla_attn",
    )(qb, qb, qb, kn, kpe, vb)


def _dil_attn_body(q_ref, k_ref, v_ref, o_ref, os_ref, ls_ref):
    blk = DA_BLOCK
    lo = lax.broadcasted_iota(I32, (blk, LANES), 1) < 64
    rowi = lax.broadcasted_iota(I32, (2 * blk, 2 * blk), 0) % blk
    colj = lax.broadcasted_iota(I32, (2 * blk, 2 * blk), 1)
    rel = rowi - colj

    for pi, (window, dil) in enumerate(DA_PATTERNS):
        n_back = window // dil
        nb = SEQ // dil // blk

        def block(t, carry, pi=pi, dil=dil, n_back=n_back, nb=nb):
            r = t // nb
            n = t % nb
            kb = jnp.maximum(n - 1, 0)
            qs = r + n * (blk * dil)
            ks = r + kb * (blk * dil)
            if dil == 1:
                qs = pl.multiple_of(qs, blk)
                ks = pl.multiple_of(ks, blk)
                q = q_ref[pl.ds(qs, blk), :]
                k2 = k_ref[pl.ds(ks, 2 * blk), :]
                v2 = v_ref[pl.ds(ks, 2 * blk), :]
            else:
                q = q_ref[pl.ds(qs, blk, stride=dil), :]
                k2 = k_ref[pl.ds(ks, 2 * blk, stride=dil), :]
                v2 = v_ref[pl.ds(ks, 2 * blk, stride=dil), :]
            zero = jnp.zeros_like(q)
            q2 = jnp.concatenate([jnp.where(lo, q, zero), jnp.where(lo, zero, q)], axis=0).astype(BF16)
            s = lax.dot_general(q2, k2.astype(BF16), NT_DIMS, preferred_element_type=F32)
            dist = (n - kb) * blk + rel
            s = jnp.where((dist >= 0) & (dist <= n_back), s, -jnp.inf)
            m = jnp.max(s, axis=-1, keepdims=True)
            e = jnp.exp(s - m)
            l = jnp.sum(e, axis=-1, keepdims=True)
            pv = jnp.dot((e / l).astype(BF16), v2.astype(BF16), preferred_element_type=F32)
            lse = m + jnp.log(l)
            o_blk = jnp.where(lo, pv[:blk], pv[blk:])
            l_blk = jnp.where(lo, jnp.broadcast_to(lse[:blk], (blk, LANES)),
                              jnp.broadcast_to(lse[blk:], (blk, LANES)))
            if dil == 1:
                os_ref[pi, pl.ds(qs, blk), :] = o_blk
                ls_ref[pi, pl.ds(qs, blk), :] = l_blk
            else:
                os_ref[pi, pl.ds(qs, blk, stride=dil), :] = o_blk
                ls_ref[pi, pl.ds(qs, blk, stride=dil), :] = l_blk
            return carry

        lax.fori_loop(0, dil * nb, block, 0)

    chunk = 512

    def combine(c, carry):
        sl = pl.ds(pl.multiple_of(c * chunk, chunk), chunk)
        l0 = ls_ref[0, sl, :]
        l1 = ls_ref[1, sl, :]
        l2 = ls_ref[2, sl, :]
        mx = jnp.maximum(jnp.maximum(l0, l1), l2)
        e0 = jnp.exp(l0 - mx)
        e1 = jnp.exp(l1 - mx)
        e2 = jnp.exp(l2 - mx)
        den = e0 + e1 + e2
        o = (e0 * os_ref[0, sl, :] + e1 * os_ref[1, sl, :] + e2 * os_ref[2, sl, :]) / den
        o_ref[sl, :] = o.astype(BF16)
        return carry

    lax.fori_loop(0, SEQ // chunk, combine, 0)


def _dil_attn(qa, ka, va):
    pairs = DA_HEADS // 2
    spec = pl.BlockSpec((SEQ, LANES), lambda b, p: (b, p))
    n_pat = len(DA_PATTERNS)
    return pl.pallas_call(
        _dil_attn_body,
        grid=(BATCH, pairs),
        in_specs=[spec, spec, spec],
        out_specs=spec,
        out_shape=jax.ShapeDtypeStruct((TOKENS, DA_WIDTH), BF16),
        scratch_shapes=[pltpu.VMEM((n_pat, SEQ, LANES), F32), pltpu.VMEM((n_pat, SEQ, LANES), F32)],
        compiler_params=_params(("arbitrary", "arbitrary")),
        name="dil_attn",
    )(qa, ka, va)


def _out_proj_body(*refs, n_in, has_bias):
    a_refs = refs[:n_in]
    w_refs = refs[n_in:2 * n_in]
    rest = refs[2 * n_in:]
    if has_bias:
        b_ref, x_ref, g_ref, o_ref = rest
    else:
        x_ref, g_ref, o_ref = rest
    y = jnp.dot(a_refs[0][...], w_refs[0][...], preferred_element_type=F32)
    for a_ref, w_ref in zip(a_refs[1:], w_refs[1:]):
        y = y + jnp.dot(a_ref[...], w_ref[...], preferred_element_type=F32)
    if has_bias:
        y = y + b_ref[...]
    o_ref[...] = x_ref[...] + g_ref[0] * y


def _out_proj(acts, weights, bias, x, mods):
    tm = ROW_TILE
    tpb = SEQ // tm
    in_specs = [pl.BlockSpec((tm, a.shape[1]), lambda i: (i, 0)) for a in acts]
    in_specs += [pl.BlockSpec(w.shape, lambda i: (0, 0)) for w in weights]
    args = list(acts) + list(weights)
    if bias is not None:
        in_specs.append(pl.BlockSpec((1, D_MODEL), lambda i: (0, 0)))
        args.append(bias)
    in_specs += [pl.BlockSpec((tm, D_MODEL), lambda i: (i, 0)), _mod_spec(2, tpb)]
    args += [x, mods]
    return pl.pallas_call(
        functools.partial(_out_proj_body, n_in=len(acts), has_bias=bias is not None),
        grid=(TOKENS // tm,),
        in_specs=in_specs,
        out_specs=pl.BlockSpec((tm, D_MODEL), lambda i: (i, 0)),
        out_shape=jax.ShapeDtypeStruct((TOKENS, D_MODEL), F32),
        compiler_params=_params(("arbitrary",)),
        name="out_proj",
    )(*args)


def _in_rec_body(x_ref, sh_ref, sc_ref, g_ref, w_ref, b_ref, gate_ref, xr_ref):
    h = _prenorm(x_ref[...], g_ref[...], sh_ref[0], sc_ref[0])
    gx = jnp.dot(h.astype(BF16), w_ref[...], preferred_element_type=F32) + b_ref[...]
    gate_ref[...] = jax.nn.gelu(gx[:, :LRU_WIDTH], approximate=True)
    xr_ref[...] = gx[:, LRU_WIDTH:]


def _in_rec(x, mods, gain, w, b):
    tm = ROW_TILE
    tpb = SEQ // tm
    row = lambda cols: pl.BlockSpec((tm, cols), lambda i: (i, 0))
    full = lambda shape: pl.BlockSpec(shape, lambda i: (0,) * len(shape))
    return pl.pallas_call(
        _in_rec_body,
        grid=(TOKENS // tm,),
        in_specs=[row(D_MODEL), _mod_spec(0, tpb), _mod_spec(1, tpb), full((1, D_MODEL)),
                  full((D_MODEL, 2 * LRU_WIDTH)), full((1, 2 * LRU_WIDTH))],
        out_specs=[row(LRU_WIDTH), row(LRU_WIDTH)],
        out_shape=[jax.ShapeDtypeStruct((TOKENS, LRU_WIDTH), F32)] * 2,
        compiler_params=_params(("arbitrary",)),
        name="in_rec",
    )(x, mods, mods, gain, w, b)


def _rglru_body(xr_ref, g_ref, cw_ref, cb_ref, wa_ref, ba_ref, wx_ref, bx_ref, lam_ref, o_ref,
                tail_ref, h_ref):
    ts = SCAN_ROWS
    width = LRU_WIDTH

    @pl.when(pl.program_id(1) == 0)
    def _():
        tail_ref[...] = jnp.zeros_like(tail_ref)
        h_ref[...] = jnp.zeros_like(h_ref)

    xr = xr_ref[...]
    tail = tail_ref[...]
    row8 = lax.broadcasted_iota(I32, (SUBLANES, width), 0)
    cw = cw_ref[...]
    xc = xr * cw[CONV_WIDTH - 1:CONV_WIDTH, :] + cb_ref[...]
    for s in range(1, CONV_WIDTH):
        rolled = pltpu.roll(xr, s, axis=0)
        fix = pltpu.roll(tail, s, axis=0)
        first = jnp.where(row8 < s, fix, rolled[:SUBLANES])
        shifted = jnp.concatenate([first, rolled[SUBLANES:]], axis=0)
        xc = xc + shifted * cw[CONV_WIDTH - 1 - s:CONV_WIDTH - s, :]
    tail_ref[...] = xr[ts - SUBLANES:, :]

    xcb = xc.astype(BF16)
    bd = LRU_BLOCK_DIM
    ra = jnp.concatenate([jnp.dot(xcb[:, n * bd:(n + 1) * bd], wa_ref[n], preferred_element_type=F32)
                          for n in range(LRU_BLOCKS)], axis=1) + ba_ref[...]
    rx = jnp.concatenate([jnp.dot(xcb[:, n * bd:(n + 1) * bd], wx_ref[n], preferred_element_type=F32)
                          for n in range(LRU_BLOCKS)], axis=1) + bx_ref[...]
    r = jax.nn.sigmoid(ra)
    ig = jax.nn.sigmoid(rx)
    z = -lam_ref[...]
    softplus = jnp.maximum(z, 0.0) + jnp.log1p(jnp.exp(-jnp.abs(z)))
    log_a = (-LRU_C) * r * softplus
    a = jnp.exp(log_a)
    bt = jnp.sqrt(-jnp.tanh(log_a) * (a * a + 1.0)) * (ig * xc)

    rowi = lax.broadcasted_iota(I32, (ts, LANES), 0)
    hs = []
    for c in range(width // LANES):
        ac = a[:, c * LANES:(c + 1) * LANES]
        bc = bt[:, c * LANES:(c + 1) * LANES]
        k = 1
        while k < ts:
            keep = rowi >= k
            a_sh = jnp.where(keep, pltpu.roll(ac, k, axis=0), 1.0)
            b_sh = jnp.where(keep, pltpu.roll(bc, k, axis=0), 0.0)
            bc = ac * b_sh + bc
            ac = ac * a_sh
            k *= 2
        hs.append(ac * h_ref[0:1, c * LANES:(c + 1) * LANES] + bc)
    h = jnp.concatenate(hs, axis=1)
    h_ref[0:1, :] = h[ts - 1:ts, :]
    o_ref[...] = (g_ref[...] * h).astype(BF16)


def _rglru(xr, gate, cw, cb, wa, ba, wx, bx, lam):
    ts = SCAN_ROWS
    nt = SEQ // ts
    row = pl.BlockSpec((ts, LRU_WIDTH), lambda b, j: (b * nt + j, 0))
    full = lambda shape: pl.BlockSpec(shape, lambda b, j: (0,) * len(shape))
    vec = full((1, LRU_WIDTH))
    wspec = full((LRU_BLOCKS, LRU_BLOCK_DIM, LRU_BLOCK_DIM))
    return pl.pallas_call(
        _rglru_body,
        grid=(BATCH, nt),
        in_specs=[row, row, full((CONV_WIDTH, LRU_WIDTH)), vec, wspec, vec, wspec, vec, vec],
        out_specs=row,
        out_shape=jax.ShapeDtypeStruct((TOKENS, LRU_WIDTH), BF16),
        scratch_shapes=[pltpu.VMEM((SUBLANES, LRU_WIDTH), F32), pltpu.VMEM((SUBLANES, LRU_WIDTH), F32)],
        compiler_params=_params(("arbitrary", "arbitrary")),
        name="rglru",
    )(xr, gate, cw, cb, wa, ba, wx, bx, lam)


def _router_body(x_ref, sh_ref, sc_ref, g_ref, wr_ref, br_ref,
                 hn_ref, idx_ref, rank_ref, gate_ref, cnt_ref, carry_ref):
    tm = ROW_TILE

    @pl.when(pl.program_id(0) == 0)
    def _():
        carry_ref[...] = jnp.zeros_like(carry_ref)

    h = _prenorm(x_ref[...], g_ref[...], sh_ref[0], sc_ref[0])
    hn_ref[...] = h
    logits = jnp.dot(h, wr_ref[...], preferred_element_type=F32,
                     precision=lax.Precision.HIGHEST) + br_ref[...]
    lane = lax.broadcasted_iota(I32, (tm, LANES), 1)
    lanef = lane.astype(F32)
    neg = -jnp.inf
    work = logits
    tops, hots = [], []
    for _ in range(TOP_K):
        m = jnp.max(work, axis=-1, keepdims=True)
        first = jnp.min(jnp.where(work == m, lanef, float(LANES)), axis=-1, keepdims=True)
        hot = lanef == first
        work = jnp.where(hot, neg, work)
        tops.append((m, first))
        hots.append(hot)
    exps = [jnp.exp(m - tops[0][0]) for m, _ in tops]
    den = exps[0] + exps[1] + exps[2] + exps[3]
    sel = jnp.zeros((tm, LANES), F32)
    for hot in hots:
        sel = jnp.where(hot, 1.0, sel)
    ri = lax.broadcasted_iota(I32, (tm, tm), 0)
    ci = lax.broadcasted_iota(I32, (tm, tm), 1)
    tril = jnp.where(ci < ri, 1.0, 0.0).astype(BF16)
    cum = jnp.dot(tril, sel.astype(BF16), preferred_element_type=F32) + carry_ref[0:1, :]
    idx_slab = jnp.zeros((tm, LANES), I32)
    rank_slab = jnp.zeros((tm, LANES), I32)
    gate_slab = jnp.zeros((tm, LANES), F32)
    for k in range(TOP_K):
        rank = jnp.sum(jnp.where(hots[k], cum, 0.0), axis=-1, keepdims=True)
        idx_slab = jnp.where(lane == k, tops[k][1].astype(I32), idx_slab)
        rank_slab = jnp.where(lane == k, rank.astype(I32), rank_slab)
        gate_slab = jnp.where(lane == k, exps[k] / den, gate_slab)
    idx_ref[...] = idx_slab
    rank_ref[...] = rank_slab
    gate_ref[...] = gate_slab
    total = carry_ref[0:1, :] + jnp.sum(sel, axis=0, keepdims=True)
    carry_ref[0:1, :] = total
    cnt_ref[...] = total


def _router(x, mods, gain, wr_p, br_p):
    tm = ROW_TILE
    tpb = SEQ // tm
    row = lambda cols: pl.BlockSpec((tm, cols), lambda i: (i, 0))
    full = lambda shape: pl.BlockSpec(shape, lambda i: (0,) * len(shape))
    return pl.pallas_call(
        _router_body,
        grid=(TOKENS // tm,),
        in_specs=[row(D_MODEL), _mod_spec(3, tpb), _mod_spec(4, tpb), full((1, D_MODEL)),
                  full((D_MODEL, LANES)), full((1, LANES))],
        out_specs=[row(D_MODEL), row(LANES), row(LANES), row(LANES), full((1, LANES))],
        out_shape=[jax.ShapeDtypeStruct((TOKENS, D_MODEL), F32),
                   jax.ShapeDtypeStruct((TOKENS, LANES), I32),
                   jax.ShapeDtypeStruct((TOKENS, LANES), I32),
                   jax.ShapeDtypeStruct((TOKENS, LANES), F32),
                   jax.ShapeDtypeStruct((1, LANES), F32)],
        scratch_shapes=[pltpu.VMEM((SUBLANES, LANES), F32)],
        compiler_params=_params(("arbitrary",)),
        name="router",
    )(x, mods, mods, gain, wr_p, br_p)


def _row_copy(src_hbm, src_row, dst_vmem, dst_row, sem):
    return pltpu.make_async_copy(src_hbm.at[pl.ds(src_row, 1), :], dst_vmem.at[pl.ds(dst_row, 1), :], sem)


def _moe_gather_body(nused_ref, tok_ref, hn_hbm, xs_ref, sem):
    bm = MOE_ROWS
    i = pl.program_id(0)

    @pl.when(i < nused_ref[0])
    def _():
        def issue(r, carry):
            _row_copy(hn_hbm, tok_ref[0, 0, r], xs_ref, r, sem).start()
            return carry

        lax.fori_loop(0, bm, issue, 0)

        def drain(r, carry):
            _row_copy(hn_hbm, 0, xs_ref, r, sem).wait()
            return carry

        lax.fori_loop(0, bm, drain, 0)

    @pl.when(i >= nused_ref[0])
    def _():
        xs_ref[...] = jnp.zeros_like(xs_ref)


def _moe_gather(n_used, row_tok, hn, n_blocks):
    bm = MOE_ROWS
    grid_spec = pltpu.PrefetchScalarGridSpec(
        num_scalar_prefetch=1,
        grid=(n_blocks,),
        in_specs=[pl.BlockSpec((1, 1, bm), lambda i, nu: (i, 0, 0), memory_space=pltpu.SMEM),
                  pl.BlockSpec(memory_space=pl.ANY)],
        out_specs=pl.BlockSpec((bm, D_MODEL), lambda i, nu: (i, 0)),
        scratch_shapes=[pltpu.SemaphoreType.DMA],
    )
    return pl.pallas_call(
        _moe_gather_body,
        grid_spec=grid_spec,
        out_shape=jax.ShapeDtypeStruct((n_blocks * bm, D_MODEL), F32),
        compiler_params=_params(("arbitrary",)),
        name="moe_gather",
    )(n_used, row_tok.reshape(n_blocks, 1, bm), hn)


def _moe_ffn_body(be_ref, nused_ref, xs_ref, w1_ref, b1_ref, w2_ref, b2_ref, ys_ref, w2x_ref):
    bm = MOE_ROWS
    i = pl.program_id(0)
    f2 = 2 * D_EXPERT

    @pl.when(i == 0)
    def _():
        w2x_ref[...] = jnp.zeros_like(w2x_ref)

    @pl.when(i < nused_ref[0])
    def _():
        prev = be_ref[jnp.maximum(i - 1, 0)]

        @pl.when((i == 0) | (be_ref[i] != prev))
        def _():
            for c in range(D_MODEL // LANES):
                w2x_ref[c, pl.ds(0, D_EXPERT, stride=2), :] = w2_ref[0, :, c * LANES:(c + 1) * LANES]

        h = jnp.dot(xs_ref[...], w1_ref[0], preferred_element_type=F32) + b1_ref[0]
        gate = jnp.minimum(h, SWIGLU_LIMIT)
        act = gate * jax.nn.sigmoid(SWIGLU_ALPHA * gate)
        nxt = pltpu.roll(h, f2 - 1, axis=1)
        lin = jnp.clip(nxt, -SWIGLU_LIMIT, SWIGLU_LIMIT) + 1.0
        even = lax.broadcasted_iota(I32, (bm, f2), 1) % 2 == 0
        prod = jnp.where(even, act * lin, 0.0)
        w2x = jnp.concatenate([w2x_ref[c] for c in range(D_MODEL // LANES)], axis=1)
        ys_ref[...] = jnp.dot(prod, w2x, preferred_element_type=F32) + b2_ref[0]

    @pl.when(i >= nused_ref[0])
    def _():
        ys_ref[...] = jnp.zeros_like(ys_ref)


def _moe_ffn(block_expert, n_used, xs, w1, b1, w2, b2, layer, n_blocks):
    bm = MOE_ROWS
    f2 = 2 * D_EXPERT
    base = layer * N_EXPERTS
    grid_spec = pltpu.PrefetchScalarGridSpec(
        num_scalar_prefetch=2,
        grid=(n_blocks,),
        in_specs=[pl.BlockSpec((bm, D_MODEL), lambda i, be, nu: (i, 0)),
                  pl.BlockSpec((1, D_MODEL, f2), lambda i, be, nu: (base + be[i], 0, 0)),
                  pl.BlockSpec((1, 1, f2), lambda i, be, nu: (base + be[i], 0, 0)),
                  pl.BlockSpec((1, D_EXPERT, D_MODEL), lambda i, be, nu: (base + be[i], 0, 0)),
                  pl.BlockSpec((1, 1, D_MODEL), lambda i, be, nu: (base + be[i], 0, 0))],
        out_specs=pl.BlockSpec((bm, D_MODEL), lambda i, be, nu: (i, 0)),
        scratch_shapes=[pltpu.VMEM((D_MODEL // LANES, f2, LANES), F32)],
    )
    return pl.pallas_call(
        _moe_ffn_body,
        grid_spec=grid_spec,
        out_shape=jax.ShapeDtypeStruct((n_blocks * bm, D_MODEL), F32),
        compiler_params=_params(("arbitrary",)),
        name="moe_ffn",
    )(block_expert, n_used, xs,
      w1.reshape(DEPTH * N_EXPERTS, D_MODEL, f2), b1.reshape(DEPTH * N_EXPERTS, 1, f2),
      w2.reshape(DEPTH * N_EXPERTS, D_EXPERT, D_MODEL), b2.reshape(DEPTH * N_EXPERTS, 1, D_MODEL))


def _moe_combine_body(dest_ref, gates_ref, x_ref, g_ref, ys_hbm, o_ref, buf_ref, sem):
    tc = COMBINE_ROWS

    def issue(r, carry):
        for k in range(TOP_K):
            _row_copy(ys_hbm, dest_ref[0, 0, r * TOP_K + k], buf_ref.at[k], r, sem).start()
        return carry

    lax.fori_loop(0, tc, issue, 0)

    def drain(r, carry):
        for k in range(TOP_K):
            _row_copy(ys_hbm, 0, buf_ref.at[k], r, sem).wait()
        return carry

    lax.fori_loop(0, tc, drain, 0)

    gates = gates_ref[...]
    y = gates[:, 0:1] * buf_ref[0]
    for k in range(1, TOP_K):
        y = y + gates[:, k:k + 1] * buf_ref[k]
    o_ref[...] = x_ref[...] + g_ref[0] * y


def _moe_combine(dest, gates, x, mods, ys):
    tc = COMBINE_ROWS
    tpb = SEQ // tc
    nt = TOKENS // tc
    return pl.pallas_call(
        _moe_combine_body,
        grid=(nt,),
        in_specs=[pl.BlockSpec((1, 1, tc * TOP_K), lambda i: (i, 0, 0), memory_space=pltpu.SMEM),
                  pl.BlockSpec((tc, LANES), lambda i: (i, 0)),
                  pl.BlockSpec((tc, D_MODEL), lambda i: (i, 0)),
                  _mod_spec(5, tpb),
                  pl.BlockSpec(memory_space=pl.ANY)],
        out_specs=pl.BlockSpec((tc, D_MODEL), lambda i: (i, 0)),
        out_shape=jax.ShapeDtypeStruct((TOKENS, D_MODEL), F32),
        scratch_shapes=[pltpu.VMEM((TOP_K, tc, D_MODEL), F32), pltpu.SemaphoreType.DMA],
        compiler_params=_params(("arbitrary",)),
        name="moe_combine",
    )(dest.reshape(nt, 1, tc * TOP_K), gates, x, mods, ys)


def _attn_layer(x, mods, gain, j, cos_t, sin_t, p):
    w_in = p["attn_w_in"][j]
    rope_cols = A_COLS + MLA_Q_RANK + MLA_KV_RANK
    w_p = jnp.concatenate(
        [w_in[:, :rope_cols],
         jnp.tile(w_in[:, rope_cols:rope_cols + MLA_HALF], (1, MLA_HEADS)),
         jnp.tile(w_in[:, rope_cols + MLA_HALF:], (1, MLA_HEADS))], axis=1).astype(BF16)
    tile_h = lambda v: jnp.tile(v, MLA_HEADS).reshape(1, -1)
    kg = p["mla_k_gain"][j]
    qg = p["mla_q_gain"][j]
    qa, ka, va, cqn, ckvn, kpe = _in_attn(
        x, mods, gain, w_p, tile_h(p["da_q_gain"][j]), tile_h(p["da_k_gain"][j]),
        p["mla_qa_gain"][j].reshape(1, -1), p["mla_kva_gain"][j].reshape(1, -1),
        tile_h(kg[MLA_NOPE:MLA_NOPE + MLA_HALF]), tile_h(kg[MLA_NOPE + MLA_HALF:]), cos_t, sin_t)

    wq = p["mla_w_uq"][j].reshape(MLA_Q_RANK, MLA_HEADS, MLA_QK)
    wq_p = jnp.concatenate(
        [wq[:, :, :MLA_NOPE].reshape(MLA_Q_RANK, -1),
         wq[:, :, MLA_NOPE:MLA_NOPE + MLA_HALF].reshape(MLA_Q_RANK, -1),
         wq[:, :, MLA_NOPE + MLA_HALF:].reshape(MLA_Q_RANK, -1)], axis=1).astype(BF16)
    wkv = p["mla_w_ukv"][j].reshape(MLA_KV_RANK, MLA_HEADS, MLA_NOPE + MLA_V)
    wkv_p = jnp.concatenate(
        [wkv[:, :, :MLA_NOPE].reshape(MLA_KV_RANK, -1),
         wkv[:, :, MLA_NOPE:].reshape(MLA_KV_RANK, -1)], axis=1).astype(BF16)
    qb, kn, vb = _mla_up(
        cqn, ckvn, wq_p, wkv_p, tile_h(qg[:MLA_NOPE]), tile_h(qg[MLA_NOPE:MLA_NOPE + MLA_HALF]),
        tile_h(qg[MLA_NOPE + MLA_HALF:]), tile_h(kg[:MLA_NOPE]), cos_t, sin_t)

    o_a = _dil_attn(qa, ka, va)
    o_b = _mla_attn(qb, kn, kpe, vb)
    w_out = p["attn_w_out"][j].astype(BF16)
    return _out_proj([o_a, o_b], [w_out[:DA_WIDTH], w_out[DA_WIDTH:]], None, x, mods)


def _rec_layer(x, mods, gain, j, p):
    gate, xr = _in_rec(x, mods, gain, p["rec_w_in"][j].astype(BF16), p["rec_b_in"][j].reshape(1, -1))
    gy = _rglru(xr, gate, p["rec_conv_w"][j], p["rec_conv_b"][j].reshape(1, -1),
                p["rec_wa"][j].astype(BF16), p["rec_ba"][j].reshape(1, -1),
                p["rec_wx"][j].astype(BF16), p["rec_bx"][j].reshape(1, -1),
                p["rec_lambda"][j].reshape(1, -1))
    return _out_proj([gy], [p["rec_w_out"][j].astype(BF16)], p["rec_b_out"][j].reshape(1, -1), x, mods)


def _moe_layer(x, mods, gain, layer, p):
    bm = MOE_ROWS
    wr_p = jnp.pad(p["moe_w_router"][layer], ((0, 0), (0, LANES - N_EXPERTS)))
    br_p = jnp.pad(p["moe_b_router"][layer], (0, LANES - N_EXPERTS), constant_values=-jnp.inf).reshape(1, LANES)
    hn, idx_slab, rank_slab, gate_slab, cnt = _router(x, mods, gain, wr_p, br_p)

    counts = cnt[0, :N_EXPERTS].astype(I32)
    padded = (counts + bm - 1) // bm * bm
    pad_ends = jnp.cumsum(padded)
    pad_starts = pad_ends - padded
    n_blocks = TOKENS * TOP_K // bm + N_EXPERTS
    dest = pad_starts[idx_slab[:, :TOP_K]] + rank_slab[:, :TOP_K]
    block_expert = jnp.minimum(
        jnp.searchsorted(pad_ends, jnp.arange(n_blocks, dtype=I32) * bm, side="right"), N_EXPERTS - 1).astype(I32)
    n_used = (pad_ends[-1:] // bm).astype(I32)
    tok = jnp.repeat(jnp.arange(TOKENS, dtype=I32), TOP_K)
    row_tok = jnp.zeros((n_blocks * bm,), I32).at[dest.reshape(-1)].set(tok)

    xs = _moe_gather(n_used, row_tok, hn, n_blocks)
    ys = _moe_ffn(block_expert, n_used, xs, p["moe_w1"], p["moe_b1"], p["moe_w2"], p["moe_b2"], layer, n_blocks)
    return _moe_combine(dest, gate_slab, x, mods, ys)


def kernel(x, c, positions, ada_w, ada_b, norm1_gain, norm2_gain, attn_w_in, da_q_gain, da_k_gain, mla_qa_gain, mla_kva_gain, mla_w_uq, mla_w_ukv, mla_q_gain, mla_k_gain, attn_w_out, rec_w_in, rec_b_in, rec_conv_w, rec_conv_b, rec_wa, rec_ba, rec_wx, rec_bx, rec_lambda, rec_w_out, rec_b_out, moe_w_router, moe_b_router, moe_w1, moe_b1, moe_w2, moe_b2):
    p = dict(attn_w_in=attn_w_in, da_q_gain=da_q_gain, da_k_gain=da_k_gain, mla_qa_gain=mla_qa_gain,
             mla_kva_gain=mla_kva_gain, mla_w_uq=mla_w_uq, mla_w_ukv=mla_w_ukv, mla_q_gain=mla_q_gain,
             mla_k_gain=mla_k_gain, attn_w_out=attn_w_out, rec_w_in=rec_w_in, rec_b_in=rec_b_in,
             rec_conv_w=rec_conv_w, rec_conv_b=rec_conv_b, rec_wa=rec_wa, rec_ba=rec_ba, rec_wx=rec_wx,
             rec_bx=rec_bx, rec_lambda=rec_lambda, rec_w_out=rec_w_out, rec_b_out=rec_b_out,
             moe_w_router=moe_w_router, moe_b_router=moe_b_router, moe_w1=moe_w1, moe_b1=moe_b1,
             moe_w2=moe_w2, moe_b2=moe_b2)
    mods_all = _ada(c, ada_w, ada_b)
    cos_t, sin_t = _rope_tables(positions)
    xt = x.reshape(TOKENS, D_MODEL)
    for layer in range(DEPTH):
        mods = mods_all[layer]
        j = layer // 2
        gain1 = norm1_gain[layer].reshape(1, D_MODEL)
        if layer % 2 == 0:
            xt = _attn_layer(xt, mods, gain1, j, cos_t, sin_t, p)
        else:
            xt = _rec_layer(xt, mods, gain1, j, p)
        xt = _moe_layer(xt, mods, norm2_gain[layer].reshape(1, D_MODEL), layer, p)
    return xt.reshape(BATCH, SEQ, D_MODEL)
```

```python
import functools

import jax
import jax.numpy as jnp
from jax import lax
from jax.experimental import pallas as pl
from jax.experimental.pallas import tpu as pltpu

F32 = jnp.float32
BF16 = jnp.bfloat16
I32 = jnp.int32

D_MODEL = 1024
BATCH = 4
SEQ = 4096
TOKENS = BATCH * SEQ
DEPTH = 4
EPS = 1e-6

DA_HEADS = 8
DA_HEAD_DIM = 64
DA_PATTERNS = ((128, 1), (512, 4), (2048, 16))
DA_BLOCK = 128
DA_WIDTH = DA_HEADS * DA_HEAD_DIM
DA_INTERLEAVE = 4

MLA_HEADS = 8
MLA_Q_RANK = 256
MLA_KV_RANK = 128
MLA_NOPE = 64
MLA_ROPE = 32
MLA_HALF = MLA_ROPE // 2
MLA_V = 64
MLA_QK = MLA_NOPE + MLA_ROPE
ROPE_THETA = 10000.0
A_COLS = 3 * DA_WIDTH

LRU_WIDTH = D_MODEL
LRU_BLOCKS = 8
LRU_BLOCK_DIM = LRU_WIDTH // LRU_BLOCKS
CONV_WIDTH = 4
LRU_C = 8.0

N_EXPERTS = 32
TOP_K = 4
D_EXPERT = D_MODEL
SWIGLU_LIMIT = 7.0
SWIGLU_ALPHA = 1.702

LANES = 128
SUBLANES = 8
ROW_CHUNKS = D_MODEL // LANES
assert ROW_CHUNKS == SUBLANES
VMEM_LIMIT = 56 * 1024 * 1024

ROW_TILE = 256
MOE_ROWS = 256
COMBINE_ROWS = 256
SCAN_ROWS = 128
MLA_TQ = 256
MLA_TK = 512
assert MLA_TK % MLA_TQ == 0

NT_DIMS = (((1,), (1,)), ((), ()))


def _params(sem, vmem=VMEM_LIMIT):
    return pltpu.CompilerParams(dimension_semantics=sem, vmem_limit_bytes=vmem)


def _prenorm(x, gain, shift, scale):
    ms = jnp.mean(x * x, axis=-1, keepdims=True)
    return (x * lax.rsqrt(ms + EPS) * gain) * (1.0 + scale) + shift


def _group_norm64(y):
    rows = y.shape[0]
    lo = lax.broadcasted_iota(I32, (rows, LANES), 1) < 64
    outs = []
    for j in range(y.shape[1] // LANES):
        c = y[:, j * LANES:(j + 1) * LANES]
        sq = c * c
        s0 = jnp.sum(jnp.where(lo, sq, 0.0), axis=-1, keepdims=True)
        s1 = jnp.sum(jnp.where(lo, 0.0, sq), axis=-1, keepdims=True)
        r = jnp.where(lo, lax.rsqrt(s0 * (1.0 / 64) + EPS), lax.rsqrt(s1 * (1.0 / 64) + EPS))
        outs.append(c * r)
    return jnp.concatenate(outs, axis=-1)


def _ada_body(c_ref, w_ref, b_ref, o_ref):
    c = c_ref[...]
    ca = c * jax.nn.sigmoid(c)
    o_ref[0] = jnp.dot(ca.astype(BF16), w_ref[0].astype(BF16), preferred_element_type=F32) + b_ref[0]


def _ada(c, ada_w, ada_b):
    c8 = jnp.pad(c, ((0, SUBLANES - BATCH), (0, 0)))
    out = pl.pallas_call(
        _ada_body,
        grid=(DEPTH, 6),
        in_specs=[pl.BlockSpec((SUBLANES, D_MODEL), lambda l, j: (0, 0)),
                  pl.BlockSpec((1, D_MODEL, D_MODEL), lambda l, j: (l, 0, j)),
                  pl.BlockSpec((1, 1, D_MODEL), lambda l, j: (l, 0, j))],
        out_specs=pl.BlockSpec((1, SUBLANES, D_MODEL), lambda l, j: (l, 0, j)),
        out_shape=jax.ShapeDtypeStruct((DEPTH, SUBLANES, 6 * D_MODEL), F32),
        compiler_params=_params(("arbitrary", "arbitrary")),
        name="ada",
    )(c8, ada_w, ada_b.reshape(DEPTH, 1, 6 * D_MODEL))
    return out[:, :BATCH].reshape(DEPTH, BATCH * 6, 1, D_MODEL)


def _mod_spec(j, rows_per_batch_tiles):
    return pl.BlockSpec((1, 1, D_MODEL), lambda i: ((i // rows_per_batch_tiles) * 6 + j, 0, 0))


def _rope_body(pos_ref, inv_ref, cos_ref, sin_ref):
    ang = pos_ref[...].astype(F32) * inv_ref[...]
    cos_ref[...] = jnp.cos(ang)
    sin_ref[...] = jnp.sin(ang)


def _rope_tables(positions):
    inv = ROPE_THETA ** (-jnp.arange(0, MLA_ROPE, 2, dtype=F32) / MLA_ROPE)
    inv_t = jnp.tile(inv, MLA_HEADS).reshape(1, LANES)
    tm = 512
    return pl.pallas_call(
        _rope_body,
        grid=(TOKENS // tm,),
        in_specs=[pl.BlockSpec((tm, 1), lambda i: (i, 0)),
                  pl.BlockSpec((1, LANES), lambda i: (0, 0))],
        out_specs=[pl.BlockSpec((tm, LANES), lambda i: (i, 0))] * 2,
        out_shape=[jax.ShapeDtypeStruct((TOKENS, LANES), F32)] * 2,
        compiler_params=_params(("arbitrary",)),
        name="rope_tables",
    )(positions.reshape(TOKENS, 1), inv_t)


IN_ATTN_COLS = A_COLS + MLA_Q_RANK + MLA_KV_RANK + 2 * LANES


def _in_attn_body(x_ref, sh_ref, sc_ref, g_ref, w_ref, gq_ref, gk_ref, gcq_ref, gckv_ref,
                  gk1_ref, gk2_ref, cos_ref, sin_ref,
                  qa_ref, ka_ref, va_ref, cq_ref, ckv_ref, kpe_ref):
    h = _prenorm(x_ref[...], g_ref[...], sh_ref[0], sc_ref[0])
    proj = jnp.dot(h.astype(BF16), w_ref[...], preferred_element_type=F32)
    w = DA_WIDTH
    qa_ref[...] = _group_norm64(proj[:, 0:w]) * gq_ref[...] * (DA_HEAD_DIM ** -0.5)
    ka_ref[...] = _group_norm64(proj[:, w:2 * w]) * gk_ref[...]
    va_ref[...] = proj[:, 2 * w:3 * w]
    o = A_COLS
    cq = proj[:, o:o + MLA_Q_RANK]
    cq_ref[...] = (cq * lax.rsqrt(jnp.mean(cq * cq, axis=-1, keepdims=True) + EPS) * gcq_ref[...]).astype(BF16)
    o += MLA_Q_RANK
    ckv = proj[:, o:o + MLA_KV_RANK]
    ckv_ref[...] = (ckv * lax.rsqrt(jnp.mean(ckv * ckv, axis=-1, keepdims=True) + EPS) * gckv_ref[...]).astype(BF16)
    o += MLA_KV_RANK
    x1 = proj[:, o:o + LANES]
    x2 = proj[:, o + LANES:o + 2 * LANES]
    ss = jnp.sum(x1 * x1 + x2 * x2, axis=-1, keepdims=True) * (1.0 / (MLA_HEADS * MLA_ROPE))
    rs = lax.rsqrt(ss + EPS)
    n1 = x1 * rs * gk1_ref[...]
    n2 = x2 * rs * gk2_ref[...]
    cos = cos_ref[...]
    sin = sin_ref[...]
    lane = lax.broadcasted_iota(I32, x1.shape, 1)
    r1 = n1 * cos - n2 * sin
    r2 = n1 * sin + n2 * cos
    kpe_ref[...] = jnp.where(lane < MLA_ROPE, r1, jnp.where(lane < 2 * MLA_ROPE, r2, 0.0)).astype(BF16)


def _in_attn(x, mods, gain, w_p, gq, gk, gcq, gckv, gk1, gk2, cos_t, sin_t):
    tm = ROW_TILE
    tpb = SEQ // tm
    row = lambda cols: pl.BlockSpec((tm, cols), lambda i: (i, 0))
    full = lambda shape: pl.BlockSpec(shape, lambda i: (0,) * len(shape))
    return pl.pallas_call(
        _in_attn_body,
        grid=(TOKENS // tm,),
        in_specs=[row(D_MODEL), _mod_spec(0, tpb), _mod_spec(1, tpb), full((1, D_MODEL)),
                  full((D_MODEL, IN_ATTN_COLS)), full((1, DA_WIDTH)), full((1, DA_WIDTH)),
                  full((1, MLA_Q_RANK)), full((1, MLA_KV_RANK)), full((1, LANES)), full((1, LANES)),
                  row(LANES), row(LANES)],
        out_specs=[row(DA_WIDTH), row(DA_WIDTH), row(DA_WIDTH), row(MLA_Q_RANK), row(MLA_KV_RANK),
                   row(LANES)],
        out_shape=[jax.ShapeDtypeStruct((TOKENS, DA_WIDTH), F32)] * 3
        + [jax.ShapeDtypeStruct((TOKENS, MLA_Q_RANK), BF16),
           jax.ShapeDtypeStruct((TOKENS, MLA_KV_RANK), BF16),
           jax.ShapeDtypeStruct((TOKENS, LANES), BF16)],
        compiler_params=_params(("arbitrary",)),
        name="in_attn",
    )(x, mods, mods, gain, w_p, gq, gk, gcq, gckv, gk1, gk2, cos_t, sin_t)


MLA_PAIRS = MLA_HEADS // 2
MLA_NW = MLA_HEADS * MLA_NOPE
MLA_Q_COLS = MLA_NW + MLA_PAIRS * LANES
MLA_CAT = 2 * LANES
LOG2E = 1.4426950408889634


def _mla_up_body(cq_ref, ckv_ref, kpe_ref, wq_ref, wkv_ref, gqn_ref, gqr_ref, gkn_ref, cos_ref, sin_ref,
                 qcat_ref, kcat_ref, vext_ref):
    rows = cq_ref.shape[0]
    q = jnp.dot(cq_ref[...], wq_ref[...], preferred_element_type=F32)
    kv = jnp.dot(ckv_ref[...], wkv_ref[...], preferred_element_type=F32)
    nw = MLA_NW
    scale = (MLA_QK ** -0.5) * LOG2E
    qn = _group_norm64(q[:, 0:nw]) * gqn_ref[...] * scale
    kn = _group_norm64(kv[:, 0:nw]) * gkn_ref[...]
    v = kv[:, nw:]
    lane = lax.broadcasted_iota(I32, (rows, LANES), 1)
    lo = lane < 64
    gi = lax.broadcasted_iota(I32, (LANES, LANES), 0)
    gj = lax.broadcasted_iota(I32, (LANES, LANES), 1)
    same = ((gi % MLA_ROPE) // MLA_HALF == (gj % MLA_ROPE) // MLA_HALF) & (gi < 64) & (gj < 64)
    gmat = jnp.where(same, 1.0, 0.0).astype(BF16)
    cos = cos_ref[...]
    sin = sin_ref[...]
    kpe = kpe_ref[...]
    for p in range(MLA_PAIRS):
        x = q[:, nw + p * LANES:nw + (p + 1) * LANES]
        ssq = x * x
        hi = ssq.astype(BF16)
        low = (ssq - hi.astype(F32)).astype(BF16)
        ss = jnp.dot(hi, gmat, preferred_element_type=F32) + jnp.dot(low, gmat, preferred_element_type=F32)
        n = x * lax.rsqrt(ss * (1.0 / MLA_ROPE) + EPS) * gqr_ref[:, p * LANES:(p + 1) * LANES]
        partner = jnp.where(lane < MLA_ROPE, pltpu.roll(n, LANES - MLA_ROPE, axis=1), pltpu.roll(n, MLA_ROPE, axis=1))
        rot = jnp.where(lane < MLA_ROPE, n * cos - partner * sin, n * cos + partner * sin)
        qr = jnp.where(lo, rot * scale, 0.0)
        base = p * MLA_CAT
        qcat_ref[:, base:base + LANES] = qn[:, p * LANES:(p + 1) * LANES].astype(BF16)
        qcat_ref[:, base + LANES:base + MLA_CAT] = qr.astype(BF16)
        kcat_ref[:, base:base + LANES] = kn[:, p * LANES:(p + 1) * LANES].astype(BF16)
        kcat_ref[:, base + LANES:base + MLA_CAT] = kpe
        vp = v[:, p * LANES:(p + 1) * LANES]
        vext_ref[:, base:base + LANES] = jnp.where(lo, vp, 1.0).astype(BF16)
        vext_ref[:, base + LANES:base + MLA_CAT] = jnp.where(lo, 1.0, vp).astype(BF16)


def _mla_up(cqn, ckvn, kpe, wq_p, wkv_p, gqn, gqr, gkn, cos_t, sin_t):
    tm = ROW_TILE
    nw = MLA_NW
    cat = MLA_PAIRS * MLA_CAT
    row = lambda cols: pl.BlockSpec((tm, cols), lambda i: (i, 0))
    full = lambda shape: pl.BlockSpec(shape, lambda i: (0,) * len(shape))
    return pl.pallas_call(
        _mla_up_body,
        grid=(TOKENS // tm,),
        in_specs=[row(MLA_Q_RANK), row(MLA_KV_RANK), row(LANES), full((MLA_Q_RANK, MLA_Q_COLS)),
                  full((MLA_KV_RANK, 2 * nw)), full((1, nw)), full((1, MLA_PAIRS * LANES)),
                  full((1, nw)), row(LANES), row(LANES)],
        out_specs=[row(cat), row(cat), row(cat)],
        out_shape=[jax.ShapeDtypeStruct((TOKENS, cat), BF16)] * 3,
        compiler_params=_params(("arbitrary",)),
        name="mla_up",
    )(cqn, ckvn, kpe, wq_p, wkv_p, gqn, gqr, gkn, cos_t, sin_t)


def _mla_attn_body(q_ref, k_ref, v_ref, o_ref):
    tq, tk = MLA_TQ, MLA_TK
    qi = pl.program_id(2)
    lane = lax.broadcasted_iota(I32, (tq, MLA_CAT), 1)
    rot = lane - LANES
    head0 = (lane < 64) | ((rot >= 0) & (rot < 64) & ((rot % MLA_ROPE) < MLA_HALF))
    head1 = ((lane >= 64) & (lane < LANES)) | ((rot >= 0) & (rot < 64) & ((rot % MLA_ROPE) >= MLA_HALF))
    q = q_ref[...]
    zero = jnp.zeros_like(q)
    q2 = jnp.concatenate([jnp.where(head0, q, zero), jnp.where(head1, q, zero)], axis=0)

    def step(ki, carry, masked):
        ks = pl.multiple_of(ki * tk, tk)
        s = lax.dot_general(q2, k_ref[pl.ds(ks, tk), :], NT_DIMS, preferred_element_type=F32)
        if masked:
            rowg = qi * tq + lax.broadcasted_iota(I32, (tq, tk), 0)
            colg = ki * tk + lax.broadcasted_iota(I32, (tq, tk), 1)
            keep = colg <= rowg
        out = []
        for h in range(2):
            m, acc = carry[h]
            sh = s[h * tq:(h + 1) * tq]
            if masked:
                sh = jnp.where(keep, sh, -jnp.inf)
            m_new = jnp.maximum(m, jnp.max(sh, axis=-1, keepdims=True))
            pb = jnp.exp2(sh - m_new).astype(BF16)
            pv = jnp.dot(pb, v_ref[pl.ds(ks, tk), h * LANES:(h + 1) * LANES], preferred_element_type=F32)
            out.append((m_new, jnp.exp2(m - m_new) * acc + pv))
        return tuple(out)

    init = tuple((jnp.full((tq, 1), -jnp.inf, F32), jnp.zeros((tq, LANES), F32)) for _ in range(2))
    kd = (qi * tq) // tk
    carry = lax.fori_loop(0, kd, lambda ki, c: step(ki, c, False), init)
    (_, acc0), (_, acc1) = step(kd, carry, True)
    half = LANES // 2
    out0 = acc0 / pltpu.roll(acc0, half, axis=1)
    out1 = acc1 / pltpu.roll(acc1, half, axis=1)
    o_ref[...] = jnp.where(lax.broadcasted_iota(I32, (tq, LANES), 1) < half, out0, out1).astype(BF16)


def _mla_attn(qcat, kcat, vext):
    tq = MLA_TQ
    nq = SEQ // tq
    return pl.pallas_call(
        _mla_attn_body,
        grid=(BATCH, MLA_PAIRS, nq),
        in_specs=[pl.BlockSpec((tq, MLA_CAT), lambda b, p, q: (b * nq + q, p)),
                  pl.BlockSpec((SEQ, MLA_CAT), lambda b, p, q: (b, p)),
                  pl.BlockSpec((SEQ, MLA_CAT), lambda b, p, q: (b, p))],
        out_specs=pl.BlockSpec((tq, LANES), lambda b, p, q: (b * nq + q, p)),
        out_shape=jax.ShapeDtypeStruct((TOKENS, MLA_HEADS * MLA_V), BF16),
        compiler_params=_params(("arbitrary", "arbitrary", "arbitrary")),
        name="mla_attn",
    )(qcat, kcat, vext)


def _dil_attn_body(q_ref, k_ref, v_ref, o_ref, os_ref, ls_ref):
    blk = DA_BLOCK
    lo = lax.broadcasted_iota(I32, (blk, LANES), 1) < 64
    rowi = lax.broadcasted_iota(I32, (2 * blk, 2 * blk), 0) % blk
    colj = lax.broadcasted_iota(I32, (2 * blk, 2 * blk), 1)
    rel = rowi - colj

    for pi, (window, dil) in enumerate(DA_PATTERNS):
        n_back = window // dil
        nb = SEQ // dil // blk

        def block(t, carry, pi=pi, dil=dil, n_back=n_back, nb=nb):
            r = t // nb
            n = t % nb
            kb = jnp.maximum(n - 1, 0)
            qs = r + n * (blk * dil)
            ks = r + kb * (blk * dil)
            if dil == 1:
                qs = pl.multiple_of(qs, blk)
                ks = pl.multiple_of(ks, blk)
                q = q_ref[pl.ds(qs, blk), :]
                k2 = k_ref[pl.ds(ks, 2 * blk), :]
                v2 = v_ref[pl.ds(ks, 2 * blk), :]
            else:
                q = q_ref[pl.ds(qs, blk, stride=dil), :]
                k2 = k_ref[pl.ds(ks, 2 * blk, stride=dil), :]
                v2 = v_ref[pl.ds(ks, 2 * blk, stride=dil), :]
            zero = jnp.zeros_like(q)
            q2 = jnp.concatenate([jnp.where(lo, q, zero), jnp.where(lo, zero, q)], axis=0).astype(BF16)
            s = lax.dot_general(q2, k2.astype(BF16), NT_DIMS, preferred_element_type=F32)
            dist = (n - kb) * blk + rel
            s = jnp.where((dist >= 0) & (dist <= n_back), s, -jnp.inf)
            m = jnp.max(s, axis=-1, keepdims=True)
            e = jnp.exp(s - m)
            l = jnp.sum(e, axis=-1, keepdims=True)
            pv = jnp.dot((e / l).astype(BF16), v2.astype(BF16), preferred_element_type=F32)
            lse = m + jnp.log(l)
            o_blk = jnp.where(lo, pv[:blk], pv[blk:])
            l_blk = jnp.where(lo, jnp.broadcast_to(lse[:blk], (blk, LANES)),
                              jnp.broadcast_to(lse[blk:], (blk, LANES)))
            if dil == 1:
                os_ref[pi, pl.ds(qs, blk), :] = o_blk
                ls_ref[pi, pl.ds(qs, blk), :] = l_blk
            else:
                os_ref[pi, pl.ds(qs, blk, stride=dil), :] = o_blk
                ls_ref[pi, pl.ds(qs, blk, stride=dil), :] = l_blk
            return carry

        def group(g, carry, block=block):
            for u in range(DA_INTERLEAVE):
                block(g * DA_INTERLEAVE + u, carry)
            return carry

        lax.fori_loop(0, dil * nb // DA_INTERLEAVE, group, 0)

    chunk = 512

    def combine(c, carry):
        sl = pl.ds(pl.multiple_of(c * chunk, chunk), chunk)
        l0 = ls_ref[0, sl, :]
        l1 = ls_ref[1, sl, :]
        l2 = ls_ref[2, sl, :]
        mx = jnp.maximum(jnp.maximum(l0, l1), l2)
        e0 = jnp.exp(l0 - mx)
        e1 = jnp.exp(l1 - mx)
        e2 = jnp.exp(l2 - mx)
        den = e0 + e1 + e2
        o = (e0 * os_ref[0, sl, :] + e1 * os_ref[1, sl, :] + e2 * os_ref[2, sl, :]) / den
        o_ref[sl, :] = o.astype(BF16)
        return carry

    lax.fori_loop(0, SEQ // chunk, combine, 0)


def _dil_attn(qa, ka, va):
    pairs = DA_HEADS // 2
    spec = pl.BlockSpec((SEQ, LANES), lambda b, p: (b, p))
    n_pat = len(DA_PATTERNS)
    return pl.pallas_call(
        _dil_attn_body,
        grid=(BATCH, pairs),
        in_specs=[spec, spec, spec],
        out_specs=spec,
        out_shape=jax.ShapeDtypeStruct((TOKENS, DA_WIDTH), BF16),
        scratch_shapes=[pltpu.VMEM((n_pat, SEQ, LANES), F32), pltpu.VMEM((n_pat, SEQ, LANES), F32)],
        compiler_params=_params(("arbitrary", "arbitrary")),
        name="dil_attn",
    )(qa, ka, va)


def _out_proj_body(*refs, n_in, has_bias):
    a_refs = refs[:n_in]
    w_refs = refs[n_in:2 * n_in]
    rest = refs[2 * n_in:]
    if has_bias:
        b_ref, x_ref, g_ref, o_ref = rest
    else:
        x_ref, g_ref, o_ref = rest
    y = jnp.dot(a_refs[0][...], w_refs[0][...], preferred_element_type=F32)
    for a_ref, w_ref in zip(a_refs[1:], w_refs[1:]):
        y = y + jnp.dot(a_ref[...], w_ref[...], preferred_element_type=F32)
    if has_bias:
        y = y + b_ref[...]
    o_ref[...] = x_ref[...] + g_ref[0] * y


def _out_proj(acts, weights, bias, x, mods):
    tm = ROW_TILE
    tpb = SEQ // tm
    in_specs = [pl.BlockSpec((tm, a.shape[1]), lambda i: (i, 0)) for a in acts]
    in_specs += [pl.BlockSpec(w.shape, lambda i: (0, 0)) for w in weights]
    args = list(acts) + list(weights)
    if bias is not None:
        in_specs.append(pl.BlockSpec((1, D_MODEL), lambda i: (0, 0)))
        args.append(bias)
    in_specs += [pl.BlockSpec((tm, D_MODEL), lambda i: (i, 0)), _mod_spec(2, tpb)]
    args += [x, mods]
    return pl.pallas_call(
        functools.partial(_out_proj_body, n_in=len(acts), has_bias=bias is not None),
        grid=(TOKENS // tm,),
        in_specs=in_specs,
        out_specs=pl.BlockSpec((tm, D_MODEL), lambda i: (i, 0)),
        out_shape=jax.ShapeDtypeStruct((TOKENS, D_MODEL), F32),
        compiler_params=_params(("arbitrary",)),
        name="out_proj",
    )(*args)


def _in_rec_body(x_ref, sh_ref, sc_ref, g_ref, w_ref, b_ref, gate_ref, xr_ref):
    h = _prenorm(x_ref[...], g_ref[...], sh_ref[0], sc_ref[0])
    gx = jnp.dot(h.astype(BF16), w_ref[...], preferred_element_type=F32) + b_ref[...]
    gate_ref[...] = jax.nn.gelu(gx[:, :LRU_WIDTH], approximate=True)
    xr_ref[...] = gx[:, LRU_WIDTH:]


def _in_rec(x, mods, gain, w, b):
    tm = ROW_TILE
    tpb = SEQ // tm
    row = lambda cols: pl.BlockSpec((tm, cols), lambda i: (i, 0))
    full = lambda shape: pl.BlockSpec(shape, lambda i: (0,) * len(shape))
    return pl.pallas_call(
        _in_rec_body,
        grid=(TOKENS // tm,),
        in_specs=[row(D_MODEL), _mod_spec(0, tpb), _mod_spec(1, tpb), full((1, D_MODEL)),
                  full((D_MODEL, 2 * LRU_WIDTH)), full((1, 2 * LRU_WIDTH))],
        out_specs=[row(LRU_WIDTH), row(LRU_WIDTH)],
        out_shape=[jax.ShapeDtypeStruct((TOKENS, LRU_WIDTH), F32)] * 2,
        compiler_params=_params(("arbitrary",)),
        name="in_rec",
    )(x, mods, mods, gain, w, b)


def _rglru_body(xr_ref, g_ref, cw_ref, cb_ref, wa_ref, ba_ref, wx_ref, bx_ref, lam_ref, o_ref,
                tail_ref, h_ref):
    ts = SCAN_ROWS
    width = LRU_WIDTH

    @pl.when(pl.program_id(1) == 0)
    def _():
        tail_ref[...] = jnp.zeros_like(tail_ref)
        h_ref[...] = jnp.zeros_like(h_ref)

    xr = xr_ref[...]
    tail = tail_ref[...]
    row8 = lax.broadcasted_iota(I32, (SUBLANES, width), 0)
    cw = cw_ref[...]
    xc = xr * cw[CONV_WIDTH - 1:CONV_WIDTH, :] + cb_ref[...]
    for s in range(1, CONV_WIDTH):
        rolled = pltpu.roll(xr, s, axis=0)
        fix = pltpu.roll(tail, s, axis=0)
        first = jnp.where(row8 < s, fix, rolled[:SUBLANES])
        shifted = jnp.concatenate([first, rolled[SUBLANES:]], axis=0)
        xc = xc + shifted * cw[CONV_WIDTH - 1 - s:CONV_WIDTH - s, :]
    tail_ref[...] = xr[ts - SUBLANES:, :]

    xcb = xc.astype(BF16)
    bd = LRU_BLOCK_DIM
    ra = jnp.concatenate([jnp.dot(xcb[:, n * bd:(n + 1) * bd], wa_ref[n], preferred_element_type=F32)
                          for n in range(LRU_BLOCKS)], axis=1) + ba_ref[...]
    rx = jnp.concatenate([jnp.dot(xcb[:, n * bd:(n + 1) * bd], wx_ref[n], preferred_element_type=F32)
                          for n in range(LRU_BLOCKS)], axis=1) + bx_ref[...]
    r = jax.nn.sigmoid(ra)
    ig = jax.nn.sigmoid(rx)
    z = -lam_ref[...]
    softplus = jnp.maximum(z, 0.0) + jnp.log1p(jnp.exp(-jnp.abs(z)))
    log_a = (-LRU_C) * r * softplus
    a = jnp.exp(log_a)
    bt = jnp.sqrt(-jnp.tanh(log_a) * (a * a + 1.0)) * (ig * xc)

    rowi = lax.broadcasted_iota(I32, (ts, LANES), 0)
    hs = []
    for c in range(width // LANES):
        ac = a[:, c * LANES:(c + 1) * LANES]
        bc = bt[:, c * LANES:(c + 1) * LANES]
        k = 1
        while k < ts:
            keep = rowi >= k
            a_sh = jnp.where(keep, pltpu.roll(ac, k, axis=0), 1.0)
            b_sh = jnp.where(keep, pltpu.roll(bc, k, axis=0), 0.0)
            bc = ac * b_sh + bc
            ac = ac * a_sh
            k *= 2
        hs.append(ac * h_ref[0:1, c * LANES:(c + 1) * LANES] + bc)
    h = jnp.concatenate(hs, axis=1)
    h_ref[0:1, :] = h[ts - 1:ts, :]
    o_ref[...] = (g_ref[...] * h).astype(BF16)


def _rglru(xr, gate, cw, cb, wa, ba, wx, bx, lam):
    ts = SCAN_ROWS
    nt = SEQ // ts
    row = pl.BlockSpec((ts, LRU_WIDTH), lambda b, j: (b * nt + j, 0))
    full = lambda shape: pl.BlockSpec(shape, lambda b, j: (0,) * len(shape))
    vec = full((1, LRU_WIDTH))
    wspec = full((LRU_BLOCKS, LRU_BLOCK_DIM, LRU_BLOCK_DIM))
    return pl.pallas_call(
        _rglru_body,
        grid=(BATCH, nt),
        in_specs=[row, row, full((CONV_WIDTH, LRU_WIDTH)), vec, wspec, vec, wspec, vec, vec],
        out_specs=row,
        out_shape=jax.ShapeDtypeStruct((TOKENS, LRU_WIDTH), BF16),
        scratch_shapes=[pltpu.VMEM((SUBLANES, LRU_WIDTH), F32), pltpu.VMEM((SUBLANES, LRU_WIDTH), F32)],
        compiler_params=_params(("arbitrary", "arbitrary")),
        name="rglru",
    )(xr, gate, cw, cb, wa, ba, wx, bx, lam)


def _router_body(x_ref, sh_ref, sc_ref, g_ref, wr_ref, br_ref,
                 hn_ref, idx_ref, rank_ref, gate_ref, cnt_ref, carry_ref):
    tm = ROW_TILE

    @pl.when(pl.program_id(0) == 0)
    def _():
        carry_ref[...] = jnp.zeros_like(carry_ref)

    h = _prenorm(x_ref[...], g_ref[...], sh_ref[0], sc_ref[0])
    _store_row_tiles(hn_ref, h)
    logits = jnp.dot(h, wr_ref[...], preferred_element_type=F32,
                     precision=lax.Precision.HIGHEST) + br_ref[...]
    lane = lax.broadcasted_iota(I32, (tm, LANES), 1)
    lanef = lane.astype(F32)
    neg = -jnp.inf
    work = logits
    tops, hots = [], []
    for _ in range(TOP_K):
        m = jnp.max(work, axis=-1, keepdims=True)
        first = jnp.min(jnp.where(work == m, lanef, float(LANES)), axis=-1, keepdims=True)
        hot = lanef == first
        work = jnp.where(hot, neg, work)
        tops.append((m, first))
        hots.append(hot)
    exps = [jnp.exp(m - tops[0][0]) for m, _ in tops]
    den = exps[0] + exps[1] + exps[2] + exps[3]
    sel = jnp.zeros((tm, LANES), F32)
    for hot in hots:
        sel = jnp.where(hot, 1.0, sel)
    ri = lax.broadcasted_iota(I32, (tm, tm), 0)
    ci = lax.broadcasted_iota(I32, (tm, tm), 1)
    tril = jnp.where(ci < ri, 1.0, 0.0).astype(BF16)
    cum = jnp.dot(tril, sel.astype(BF16), preferred_element_type=F32) + carry_ref[0:1, :]
    idx_slab = jnp.zeros((tm, LANES), I32)
    rank_slab = jnp.zeros((tm, LANES), I32)
    gate_slab = jnp.zeros((tm, LANES), F32)
    for k in range(TOP_K):
        rank = jnp.sum(jnp.where(hots[k], cum, 0.0), axis=-1, keepdims=True)
        idx_slab = jnp.where(lane == k, tops[k][1].astype(I32), idx_slab)
        rank_slab = jnp.where(lane == k, rank.astype(I32), rank_slab)
        gate_slab = jnp.where(lane == k, exps[k] / den, gate_slab)
    idx_ref[...] = idx_slab
    rank_ref[...] = rank_slab
    gate_ref[...] = gate_slab
    total = carry_ref[0:1, :] + jnp.sum(sel, axis=0, keepdims=True)
    carry_ref[0:1, :] = total
    cnt_ref[...] = total


def _router(x, mods, gain, wr_p, br_p):
    tm = ROW_TILE
    tpb = SEQ // tm
    row = lambda cols: pl.BlockSpec((tm, cols), lambda i: (i, 0))
    full = lambda shape: pl.BlockSpec(shape, lambda i: (0,) * len(shape))
    return pl.pallas_call(
        _router_body,
        grid=(TOKENS // tm,),
        in_specs=[row(D_MODEL), _mod_spec(3, tpb), _mod_spec(4, tpb), full((1, D_MODEL)),
                  full((D_MODEL, LANES)), full((1, LANES))],
        out_specs=[pl.BlockSpec((tm * ROW_CHUNKS, LANES), lambda i: (i, 0)),
                   row(LANES), row(LANES), row(LANES), full((1, LANES))],
        out_shape=[jax.ShapeDtypeStruct((TOKENS * ROW_CHUNKS, LANES), F32),
                   jax.ShapeDtypeStruct((TOKENS, LANES), I32),
                   jax.ShapeDtypeStruct((TOKENS, LANES), I32),
                   jax.ShapeDtypeStruct((TOKENS, LANES), F32),
                   jax.ShapeDtypeStruct((1, LANES), F32)],
        scratch_shapes=[pltpu.VMEM((SUBLANES, LANES), F32)],
        compiler_params=_params(("arbitrary",)),
        name="router",
    )(x, mods, mods, gain, wr_p, br_p)


def _tile_copy(src, src_row, dst, dst_row, sem):
    s0 = pl.multiple_of(src_row * ROW_CHUNKS, ROW_CHUNKS)
    d0 = pl.multiple_of(dst_row * ROW_CHUNKS, ROW_CHUNKS)
    return pltpu.make_async_copy(src.at[pl.ds(s0, ROW_CHUNKS), :], dst.at[pl.ds(d0, ROW_CHUNKS), :], sem)


def _store_row_tiles(ref, val):
    rows = val.shape[0]
    for c in range(ROW_CHUNKS):
        ref[pl.ds(c, rows, stride=ROW_CHUNKS), :] = val[:, c * LANES:(c + 1) * LANES]


def _load_row_tiles(ref, rows):
    return jnp.concatenate([ref[pl.ds(c, rows, stride=ROW_CHUNKS), :] for c in range(ROW_CHUNKS)], axis=1)


def _moe_dispatch_body(padbase_ref, npad_ref, nused_ref, dest_ref, hn_ref, xs_hbm, zero_ref, sem, *, n_blocks):
    td = ROW_TILE
    block_rows = MOE_ROWS * ROW_CHUNKS

    @pl.when(pl.program_id(0) == 0)
    def _():
        zero_ref[...] = jnp.zeros_like(zero_ref)

        def per_expert(e, carry, wait):
            def per_row(r, c):
                cp = _tile_copy(zero_ref, 0, xs_hbm, padbase_ref[e] + r, sem)
                if wait:
                    cp.wait()
                else:
                    cp.start()
                return c

            return lax.fori_loop(0, npad_ref[e], per_row, carry)

        def per_block(b, carry, wait):
            start = pl.multiple_of(b * block_rows, block_rows)
            cp = pltpu.make_async_copy(zero_ref, xs_hbm.at[pl.ds(start, block_rows), :], sem)
            if wait:
                cp.wait()
            else:
                cp.start()
            return carry

        for wait in (False, True):
            lax.fori_loop(0, N_EXPERTS, functools.partial(per_expert, wait=wait), 0)
            lax.fori_loop(nused_ref[0], n_blocks, functools.partial(per_block, wait=wait), 0)

    def issue(r, carry):
        for k in range(TOP_K):
            _tile_copy(hn_ref, r, xs_hbm, dest_ref[0, 0, r * TOP_K + k], sem).start()
        return carry

    lax.fori_loop(0, td, issue, 0, unroll=8)

    def drain(r, carry):
        for k in range(TOP_K):
            _tile_copy(hn_ref, r, xs_hbm, 0, sem).wait()
        return carry

    lax.fori_loop(0, td, drain, 0, unroll=8)


def _moe_dispatch(pad_base, n_pad, n_used, dest, hn_tiles, n_blocks):
    td = ROW_TILE
    nt = TOKENS // td
    grid_spec = pltpu.PrefetchScalarGridSpec(
        num_scalar_prefetch=3,
        grid=(nt,),
        in_specs=[pl.BlockSpec((1, 1, td * TOP_K), lambda i, *_: (i, 0, 0), memory_space=pltpu.SMEM),
                  pl.BlockSpec((td * ROW_CHUNKS, LANES), lambda i, *_: (i, 0))],
        out_specs=pl.BlockSpec(memory_space=pl.ANY),
        scratch_shapes=[pltpu.VMEM((MOE_ROWS * ROW_CHUNKS, LANES), F32), pltpu.SemaphoreType.DMA],
    )
    return pl.pallas_call(
        functools.partial(_moe_dispatch_body, n_blocks=n_blocks),
        grid_spec=grid_spec,
        out_shape=jax.ShapeDtypeStruct((n_blocks * MOE_ROWS * ROW_CHUNKS, LANES), F32),
        compiler_params=_params(("arbitrary",)),
        name="moe_dispatch",
    )(pad_base, n_pad, n_used, dest.reshape(nt, 1, td * TOP_K), hn_tiles)


def _moe_ffn_body(be_ref, nused_ref, xs_ref, w1_ref, b1_ref, w2_ref, b2_ref, ys_ref, w2x_ref):
    bm = MOE_ROWS
    i = pl.program_id(0)
    f2 = 2 * D_EXPERT

    @pl.when(i == 0)
    def _():
        w2x_ref[...] = jnp.zeros_like(w2x_ref)

    @pl.when(i < nused_ref[0])
    def _():
        prev = be_ref[jnp.maximum(i - 1, 0)]

        @pl.when((i == 0) | (be_ref[i] != prev))
        def _():
            for c in range(D_MODEL // LANES):
                w2x_ref[c, pl.ds(0, D_EXPERT, stride=2), :] = w2_ref[0, :, c * LANES:(c + 1) * LANES]

        h = jnp.dot(_load_row_tiles(xs_ref, bm), w1_ref[0], preferred_element_type=F32) + b1_ref[0]
        gate = jnp.minimum(h, SWIGLU_LIMIT)
        act = gate * jax.nn.sigmoid(SWIGLU_ALPHA * gate)
        nxt = pltpu.roll(h, f2 - 1, axis=1)
        lin = jnp.clip(nxt, -SWIGLU_LIMIT, SWIGLU_LIMIT) + 1.0
        even = lax.broadcasted_iota(I32, (bm, f2), 1) % 2 == 0
        prod = jnp.where(even, act * lin, 0.0)
        w2x = jnp.concatenate([w2x_ref[c] for c in range(D_MODEL // LANES)], axis=1)
        _store_row_tiles(ys_ref, jnp.dot(prod, w2x, preferred_element_type=F32) + b2_ref[0])

    @pl.when(i >= nused_ref[0])
    def _():
        ys_ref[...] = jnp.zeros_like(ys_ref)


def _moe_ffn(block_expert, n_used, xs, w1, b1, w2, b2, layer, n_blocks):
    bm = MOE_ROWS
    f2 = 2 * D_EXPERT
    base = layer * N_EXPERTS
    grid_spec = pltpu.PrefetchScalarGridSpec(
        num_scalar_prefetch=2,
        grid=(n_blocks,),
        in_specs=[pl.BlockSpec((bm * ROW_CHUNKS, LANES), lambda i, be, nu: (jnp.minimum(i, nu[0] - 1), 0)),
                  pl.BlockSpec((1, D_MODEL, f2), lambda i, be, nu: (base + be[i], 0, 0)),
                  pl.BlockSpec((1, 1, f2), lambda i, be, nu: (base + be[i], 0, 0)),
                  pl.BlockSpec((1, D_EXPERT, D_MODEL), lambda i, be, nu: (base + be[i], 0, 0)),
                  pl.BlockSpec((1, 1, D_MODEL), lambda i, be, nu: (base + be[i], 0, 0))],
        out_specs=pl.BlockSpec((bm * ROW_CHUNKS, LANES), lambda i, be, nu: (i, 0)),
        scratch_shapes=[pltpu.VMEM((D_MODEL // LANES, f2, LANES), F32)],
    )
    return pl.pallas_call(
        _moe_ffn_body,
        grid_spec=grid_spec,
        out_shape=jax.ShapeDtypeStruct((n_blocks * bm * ROW_CHUNKS, LANES), F32),
        compiler_params=_params(("arbitrary",)),
        name="moe_ffn",
    )(block_expert, n_used, xs,
      w1.reshape(DEPTH * N_EXPERTS, D_MODEL, f2), b1.reshape(DEPTH * N_EXPERTS, 1, f2),
      w2.reshape(DEPTH * N_EXPERTS, D_EXPERT, D_MODEL), b2.reshape(DEPTH * N_EXPERTS, 1, D_MODEL))


def _moe_combine_body(dest_ref, dnext_ref, gates_ref, x_ref, g_ref, ys_hbm, o_ref, buf0_ref, buf1_ref, sem):
    tc = COMBINE_ROWS
    i = pl.program_id(0)
    bufs = (buf0_ref, buf1_ref)

    def issue(d_ref, slot):
        def body(r, carry):
            for k in range(TOP_K):
                _tile_copy(ys_hbm, d_ref[0, 0, r * TOP_K + k], bufs[slot].at[k], r, sem.at[slot]).start()
            return carry

        lax.fori_loop(0, tc, body, 0, unroll=8)

    def drain(slot):
        def body(r, carry):
            for k in range(TOP_K):
                _tile_copy(ys_hbm, 0, bufs[slot].at[k], r, sem.at[slot]).wait()
            return carry

        lax.fori_loop(0, tc, body, 0, unroll=8)

    def finish(slot):
        drain(slot)
        gates = gates_ref[...]
        gate = g_ref[0]
        for c in range(ROW_CHUNKS):
            rows = pl.ds(c, tc, stride=ROW_CHUNKS)
            y = gates[:, 0:1] * bufs[slot][0, rows, :]
            for k in range(1, TOP_K):
                y = y + gates[:, k:k + 1] * bufs[slot][k, rows, :]
            cols = slice(c * LANES, (c + 1) * LANES)
            o_ref[:, cols] = x_ref[:, cols] + gate[:, cols] * y

    @pl.when(i == 0)
    def _():
        issue(dest_ref, 0)

    for slot in range(2):
        @pl.when(i % 2 == slot)
        def _(slot=slot):
            @pl.when(i + 1 < pl.num_programs(0))
            def _():
                issue(dnext_ref, 1 - slot)

            finish(slot)


def _moe_combine(dest, gates, x, mods, ys):
    tc = COMBINE_ROWS
    tpb = SEQ // tc
    nt = TOKENS // tc
    dest3 = dest.reshape(nt, 1, tc * TOP_K)
    buf = pltpu.VMEM((TOP_K, tc * ROW_CHUNKS, LANES), F32)
    return pl.pallas_call(
        _moe_combine_body,
        grid=(nt,),
        in_specs=[pl.BlockSpec((1, 1, tc * TOP_K), lambda i: (i, 0, 0), memory_space=pltpu.SMEM),
                  pl.BlockSpec((1, 1, tc * TOP_K), lambda i: (jnp.minimum(i + 1, nt - 1), 0, 0),
                               memory_space=pltpu.SMEM),
                  pl.BlockSpec((tc, LANES), lambda i: (i, 0)),
                  pl.BlockSpec((tc, D_MODEL), lambda i: (i, 0)),
                  _mod_spec(5, tpb),
                  pl.BlockSpec(memory_space=pl.ANY)],
        out_specs=pl.BlockSpec((tc, D_MODEL), lambda i: (i, 0)),
        out_shape=jax.ShapeDtypeStruct((TOKENS, D_MODEL), F32),
        scratch_shapes=[buf, buf, pltpu.SemaphoreType.DMA((2,))],
        compiler_params=_params(("arbitrary",)),
        name="moe_combine",
    )(dest3, dest3, gates, x, mods, ys)


def _attn_layer(x, mods, gain, j, cos_t, sin_t, p):
    w_in = p["attn_w_in"][j]
    rope_cols = A_COLS + MLA_Q_RANK + MLA_KV_RANK
    w_p = jnp.concatenate(
        [w_in[:, :rope_cols],
         jnp.tile(w_in[:, rope_cols:rope_cols + MLA_HALF], (1, MLA_HEADS)),
         jnp.tile(w_in[:, rope_cols + MLA_HALF:], (1, MLA_HEADS))], axis=1).astype(BF16)
    tile_h = lambda v: jnp.tile(v, MLA_HEADS).reshape(1, -1)
    kg = p["mla_k_gain"][j]
    qg = p["mla_q_gain"][j]
    qa, ka, va, cqn, ckvn, kpe = _in_attn(
        x, mods, gain, w_p, tile_h(p["da_q_gain"][j]), tile_h(p["da_k_gain"][j]),
        p["mla_qa_gain"][j].reshape(1, -1), p["mla_kva_gain"][j].reshape(1, -1),
        tile_h(kg[MLA_NOPE:MLA_NOPE + MLA_HALF]), tile_h(kg[MLA_NOPE + MLA_HALF:]), cos_t, sin_t)

    wq = p["mla_w_uq"][j].reshape(MLA_Q_RANK, MLA_HEADS, MLA_QK)
    first = slice(MLA_NOPE, MLA_NOPE + MLA_HALF)
    second = slice(MLA_NOPE + MLA_HALF, MLA_QK)
    pad_cols = jnp.zeros((MLA_Q_RANK, LANES - 2 * MLA_ROPE), F32)
    rot_blocks = [jnp.concatenate([wq[:, 2 * q, first], wq[:, 2 * q + 1, first],
                                   wq[:, 2 * q, second], wq[:, 2 * q + 1, second], pad_cols], axis=1)
                  for q in range(MLA_PAIRS)]
    wq_p = jnp.concatenate([wq[:, :, :MLA_NOPE].reshape(MLA_Q_RANK, -1)] + rot_blocks, axis=1).astype(BF16)
    gqr = jnp.tile(jnp.concatenate([qg[first], qg[first], qg[second], qg[second],
                                    jnp.zeros((LANES - 2 * MLA_ROPE,), F32)]), MLA_PAIRS).reshape(1, -1)
    wkv = p["mla_w_ukv"][j].reshape(MLA_KV_RANK, MLA_HEADS, MLA_NOPE + MLA_V)
    wkv_p = jnp.concatenate(
        [wkv[:, :, :MLA_NOPE].reshape(MLA_KV_RANK, -1),
         wkv[:, :, MLA_NOPE:].reshape(MLA_KV_RANK, -1)], axis=1).astype(BF16)
    qcat, kcat, vext = _mla_up(cqn, ckvn, kpe, wq_p, wkv_p, tile_h(qg[:MLA_NOPE]), gqr,
                               tile_h(kg[:MLA_NOPE]), cos_t, sin_t)

    o_a = _dil_attn(qa, ka, va)
    o_b = _mla_attn(qcat, kcat, vext)
    w_out = p["attn_w_out"][j].astype(BF16)
    return _out_proj([o_a, o_b], [w_out[:DA_WIDTH], w_out[DA_WIDTH:]], None, x, mods)


def _rec_layer(x, mods, gain, j, p):
    gate, xr = _in_rec(x, mods, gain, p["rec_w_in"][j].astype(BF16), p["rec_b_in"][j].reshape(1, -1))
    gy = _rglru(xr, gate, p["rec_conv_w"][j], p["rec_conv_b"][j].reshape(1, -1),
                p["rec_wa"][j].astype(BF16), p["rec_ba"][j].reshape(1, -1),
                p["rec_wx"][j].astype(BF16), p["rec_bx"][j].reshape(1, -1),
                p["rec_lambda"][j].reshape(1, -1))
    return _out_proj([gy], [p["rec_w_out"][j].astype(BF16)], p["rec_b_out"][j].reshape(1, -1), x, mods)


def _moe_layer(x, mods, gain, layer, p):
    bm = MOE_ROWS
    wr_p = jnp.pad(p["moe_w_router"][layer], ((0, 0), (0, LANES - N_EXPERTS)))
    br_p = jnp.pad(p["moe_b_router"][layer], (0, LANES - N_EXPERTS), constant_values=-jnp.inf).reshape(1, LANES)
    hn, idx_slab, rank_slab, gate_slab, cnt = _router(x, mods, gain, wr_p, br_p)

    counts = cnt[0, :N_EXPERTS].astype(I32)
    padded = (counts + bm - 1) // bm * bm
    pad_ends = jnp.cumsum(padded)
    pad_starts = pad_ends - padded
    n_blocks = TOKENS * TOP_K // bm + N_EXPERTS
    experts = jnp.arange(N_EXPERTS, dtype=I32)
    idx = idx_slab[:, :TOP_K]
    dest = rank_slab[:, :TOP_K] + jnp.sum(
        jnp.where(idx[:, :, None] == experts, pad_starts, 0), axis=-1, dtype=I32)
    block_start = jnp.arange(n_blocks, dtype=I32) * bm
    block_expert = jnp.minimum(
        jnp.sum(pad_ends[None, :] <= block_start[:, None], axis=-1, dtype=I32), N_EXPERTS - 1)
    n_used = (pad_ends[-1:] // bm).astype(I32)

    xs = _moe_dispatch(pad_starts + counts, padded - counts, n_used, dest, hn, n_blocks)
    ys = _moe_ffn(block_expert, n_used, xs, p["moe_w1"], p["moe_b1"], p["moe_w2"], p["moe_b2"], layer, n_blocks)
    return _moe_combine(dest, gate_slab, x, mods, ys)


def kernel(x, c, positions, ada_w, ada_b, norm1_gain, norm2_gain, attn_w_in, da_q_gain, da_k_gain, mla_qa_gain, mla_kva_gain, mla_w_uq, mla_w_ukv, mla_q_gain, mla_k_gain, attn_w_out, rec_w_in, rec_b_in, rec_conv_w, rec_conv_b, rec_wa, rec_ba, rec_wx, rec_bx, rec_lambda, rec_w_out, rec_b_out, moe_w_router, moe_b_router, moe_w1, moe_b1, moe_w2, moe_b2):
    p = dict(attn_w_in=attn_w_in, da_q_gain=da_q_gain, da_k_gain=da_k_gain, mla_qa_gain=mla_qa_gain,
             mla_kva_gain=mla_kva_gain, mla_w_uq=mla_w_uq, mla_w_ukv=mla_w_ukv, mla_q_gain=mla_q_gain,
             mla_k_gain=mla_k_gain, attn_w_out=attn_w_out, rec_w_in=rec_w_in, rec_b_in=rec_b_in,
             rec_conv_w=rec_conv_w, rec_conv_b=rec_conv_b, rec_wa=rec_wa, rec_ba=rec_ba, rec_wx=rec_wx,
             rec_bx=rec_bx, rec_lambda=rec_lambda, rec_w_out=rec_w_out, rec_b_out=rec_b_out,
             moe_w_router=moe_w_router, moe_b_router=moe_b_router, moe_w1=moe_w1, moe_b1=moe_b1,
             moe_w2=moe_w2, moe_b2=moe_b2)
    mods_all = _ada(c, ada_w, ada_b)
    cos_t, sin_t = _rope_tables(positions)
    xt = x.reshape(TOKENS, D_MODEL)
    for layer in range(DEPTH):
        mods = mods_all[layer]
        j = layer // 2
        gain1 = norm1_gain[layer].reshape(1, D_MODEL)
        if layer % 2 == 0:
            xt = _attn_layer(xt, mods, gain1, j, cos_t, sin_t, p)
        else:
            xt = _rec_layer(xt, mods, gain1, j, p)
        xt = _moe_layer(xt, mods, norm2_gain[layer].reshape(1, D_MODEL), layer, p)
    return xt.reshape(BATCH, SEQ, D_MODEL)
```

```python
import functools

import jax
import jax.numpy as jnp
from jax import lax
from jax.experimental import pallas as pl
from jax.experimental.pallas import tpu as pltpu

F32 = jnp.float32
BF16 = jnp.bfloat16
I32 = jnp.int32

D_MODEL = 1024
BATCH = 4
SEQ = 4096
TOKENS = BATCH * SEQ
DEPTH = 4
EPS = 1e-6

DA_HEADS = 8
DA_HEAD_DIM = 64
DA_PATTERNS = ((128, 1), (512, 4), (2048, 16))
DA_BLOCK = 128
DA_WIDTH = DA_HEADS * DA_HEAD_DIM
DA_INTERLEAVE = 8

MLA_HEADS = 8
MLA_Q_RANK = 256
MLA_KV_RANK = 128
MLA_NOPE = 64
MLA_ROPE = 32
MLA_HALF = MLA_ROPE // 2
MLA_V = 64
MLA_QK = MLA_NOPE + MLA_ROPE
ROPE_THETA = 10000.0
A_COLS = 3 * DA_WIDTH

LRU_WIDTH = D_MODEL
LRU_BLOCKS = 8
LRU_BLOCK_DIM = LRU_WIDTH // LRU_BLOCKS
CONV_WIDTH = 4
LRU_C = 8.0

N_EXPERTS = 32
TOP_K = 4
D_EXPERT = D_MODEL
SWIGLU_LIMIT = 7.0
SWIGLU_ALPHA = 1.702

LANES = 128
SUBLANES = 8
ROW_CHUNKS = D_MODEL // LANES
assert ROW_CHUNKS == SUBLANES
VMEM_LIMIT = 56 * 1024 * 1024

ROW_TILE = 256
MOE_ROWS = 256
COMBINE_ROWS = 256
SCAN_ROWS = 128
MLA_TQ = 256
MLA_TK = 512
assert MLA_TK % MLA_TQ == 0

NT_DIMS = (((1,), (1,)), ((), ()))


def _params(sem, vmem=VMEM_LIMIT):
    return pltpu.CompilerParams(dimension_semantics=sem, vmem_limit_bytes=vmem)


def _prenorm(x, gain, shift, scale):
    ms = jnp.mean(x * x, axis=-1, keepdims=True)
    return (x * lax.rsqrt(ms + EPS) * gain) * (1.0 + scale) + shift


def _group_norm64(y):
    rows = y.shape[0]
    lo = lax.broadcasted_iota(I32, (rows, LANES), 1) < 64
    outs = []
    for j in range(y.shape[1] // LANES):
        c = y[:, j * LANES:(j + 1) * LANES]
        sq = c * c
        s0 = jnp.sum(jnp.where(lo, sq, 0.0), axis=-1, keepdims=True)
        s1 = jnp.sum(jnp.where(lo, 0.0, sq), axis=-1, keepdims=True)
        r = jnp.where(lo, lax.rsqrt(s0 * (1.0 / 64) + EPS), lax.rsqrt(s1 * (1.0 / 64) + EPS))
        outs.append(c * r)
    return jnp.concatenate(outs, axis=-1)


def _ada_body(c_ref, w_ref, b_ref, o_ref):
    c = c_ref[...]
    ca = c * jax.nn.sigmoid(c)
    o_ref[0] = jnp.dot(ca.astype(BF16), w_ref[0].astype(BF16), preferred_element_type=F32) + b_ref[0]


def _ada(c, ada_w, ada_b):
    c8 = jnp.pad(c, ((0, SUBLANES - BATCH), (0, 0)))
    out = pl.pallas_call(
        _ada_body,
        grid=(DEPTH, 6),
        in_specs=[pl.BlockSpec((SUBLANES, D_MODEL), lambda l, j: (0, 0)),
                  pl.BlockSpec((1, D_MODEL, D_MODEL), lambda l, j: (l, 0, j)),
                  pl.BlockSpec((1, 1, D_MODEL), lambda l, j: (l, 0, j))],
        out_specs=pl.BlockSpec((1, SUBLANES, D_MODEL), lambda l, j: (l, 0, j)),
        out_shape=jax.ShapeDtypeStruct((DEPTH, SUBLANES, 6 * D_MODEL), F32),
        compiler_params=_params(("arbitrary", "arbitrary")),
        name="ada",
    )(c8, ada_w, ada_b.reshape(DEPTH, 1, 6 * D_MODEL))
    return out[:, :BATCH].reshape(DEPTH, BATCH * 6, 1, D_MODEL)


def _mod_spec(j, rows_per_batch_tiles):
    return pl.BlockSpec((1, 1, D_MODEL), lambda i: ((i // rows_per_batch_tiles) * 6 + j, 0, 0))


def _rope_body(pos_ref, inv_ref, cos_ref, sin_ref):
    ang = pos_ref[...].astype(F32) * inv_ref[...]
    cos_ref[...] = jnp.cos(ang)
    sin_ref[...] = jnp.sin(ang)


def _rope_tables(positions):
    inv = ROPE_THETA ** (-jnp.arange(0, MLA_ROPE, 2, dtype=F32) / MLA_ROPE)
    inv_t = jnp.tile(inv, MLA_HEADS).reshape(1, LANES)
    tm = 512
    return pl.pallas_call(
        _rope_body,
        grid=(TOKENS // tm,),
        in_specs=[pl.BlockSpec((tm, 1), lambda i: (i, 0)),
                  pl.BlockSpec((1, LANES), lambda i: (0, 0))],
        out_specs=[pl.BlockSpec((tm, LANES), lambda i: (i, 0))] * 2,
        out_shape=[jax.ShapeDtypeStruct((TOKENS, LANES), F32)] * 2,
        compiler_params=_params(("arbitrary",)),
        name="rope_tables",
    )(positions.reshape(TOKENS, 1), inv_t)


IN_ATTN_COLS = A_COLS + MLA_Q_RANK + MLA_KV_RANK + 2 * LANES


def _in_attn_body(x_ref, sh_ref, sc_ref, g_ref, w_ref, gq_ref, gk_ref, gcq_ref, gckv_ref,
                  gk1_ref, gk2_ref, cos_ref, sin_ref,
                  qa_ref, ka_ref, va_ref, cq_ref, ckv_ref, kpe_ref):
    h = _prenorm(x_ref[...], g_ref[...], sh_ref[0], sc_ref[0])
    proj = jnp.dot(h.astype(BF16), w_ref[...], preferred_element_type=F32)
    w = DA_WIDTH
    qa_ref[...] = _group_norm64(proj[:, 0:w]) * gq_ref[...] * (DA_HEAD_DIM ** -0.5)
    ka_ref[...] = _group_norm64(proj[:, w:2 * w]) * gk_ref[...]
    va_ref[...] = proj[:, 2 * w:3 * w]
    o = A_COLS
    cq = proj[:, o:o + MLA_Q_RANK]
    cq_ref[...] = (cq * lax.rsqrt(jnp.mean(cq * cq, axis=-1, keepdims=True) + EPS) * gcq_ref[...]).astype(BF16)
    o += MLA_Q_RANK
    ckv = proj[:, o:o + MLA_KV_RANK]
    ckv_ref[...] = (ckv * lax.rsqrt(jnp.mean(ckv * ckv, axis=-1, keepdims=True) + EPS) * gckv_ref[...]).astype(BF16)
    o += MLA_KV_RANK
    x1 = proj[:, o:o + LANES]
    x2 = proj[:, o + LANES:o + 2 * LANES]
    ss = jnp.sum(x1 * x1 + x2 * x2, axis=-1, keepdims=True) * (1.0 / (MLA_HEADS * MLA_ROPE))
    rs = lax.rsqrt(ss + EPS)
    n1 = x1 * rs * gk1_ref[...]
    n2 = x2 * rs * gk2_ref[...]
    cos = cos_ref[...]
    sin = sin_ref[...]
    lane = lax.broadcasted_iota(I32, x1.shape, 1)
    r1 = n1 * cos - n2 * sin
    r2 = n1 * sin + n2 * cos
    kpe_ref[...] = jnp.where(lane < MLA_ROPE, r1, jnp.where(lane < 2 * MLA_ROPE, r2, 0.0)).astype(BF16)


def _in_attn(x, mods, gain, w_p, gq, gk, gcq, gckv, gk1, gk2, cos_t, sin_t):
    tm = ROW_TILE
    tpb = SEQ // tm
    row = lambda cols: pl.BlockSpec((tm, cols), lambda i: (i, 0))
    full = lambda shape: pl.BlockSpec(shape, lambda i: (0,) * len(shape))
    return pl.pallas_call(
        _in_attn_body,
        grid=(TOKENS // tm,),
        in_specs=[row(D_MODEL), _mod_spec(0, tpb), _mod_spec(1, tpb), full((1, D_MODEL)),
                  full((D_MODEL, IN_ATTN_COLS)), full((1, DA_WIDTH)), full((1, DA_WIDTH)),
                  full((1, MLA_Q_RANK)), full((1, MLA_KV_RANK)), full((1, LANES)), full((1, LANES)),
                  row(LANES), row(LANES)],
        out_specs=[row(DA_WIDTH), row(DA_WIDTH), row(DA_WIDTH), row(MLA_Q_RANK), row(MLA_KV_RANK),
                   row(LANES)],
        out_shape=[jax.ShapeDtypeStruct((TOKENS, DA_WIDTH), F32)] * 3
        + [jax.ShapeDtypeStruct((TOKENS, MLA_Q_RANK), BF16),
           jax.ShapeDtypeStruct((TOKENS, MLA_KV_RANK), BF16),
           jax.ShapeDtypeStruct((TOKENS, LANES), BF16)],
        compiler_params=_params(("arbitrary",)),
        name="in_attn",
    )(x, mods, mods, gain, w_p, gq, gk, gcq, gckv, gk1, gk2, cos_t, sin_t)


MLA_PAIRS = MLA_HEADS // 2
MLA_NW = MLA_HEADS * MLA_NOPE
MLA_Q_COLS = MLA_NW + MLA_PAIRS * LANES
MLA_CAT = 2 * LANES
LOG2E = 1.4426950408889634


def _mla_up_body(cq_ref, ckv_ref, kpe_ref, wq_ref, wkv_ref, gqn_ref, gqr_ref, gkn_ref, cos_ref, sin_ref,
                 qcat_ref, kcat_ref, vext_ref):
    rows = cq_ref.shape[0]
    q = jnp.dot(cq_ref[...], wq_ref[...], preferred_element_type=F32)
    kv = jnp.dot(ckv_ref[...], wkv_ref[...], preferred_element_type=F32)
    nw = MLA_NW
    scale = (MLA_QK ** -0.5) * LOG2E
    qn = _group_norm64(q[:, 0:nw]) * gqn_ref[...] * scale
    kn = _group_norm64(kv[:, 0:nw]) * gkn_ref[...]
    v = kv[:, nw:]
    lane = lax.broadcasted_iota(I32, (rows, LANES), 1)
    lo = lane < 64
    gi = lax.broadcasted_iota(I32, (LANES, LANES), 0)
    gj = lax.broadcasted_iota(I32, (LANES, LANES), 1)
    same = ((gi % MLA_ROPE) // MLA_HALF == (gj % MLA_ROPE) // MLA_HALF) & (gi < 64) & (gj < 64)
    gmat = jnp.where(same, 1.0, 0.0).astype(BF16)
    cos = cos_ref[...]
    sin = sin_ref[...]
    kpe = kpe_ref[...]
    for p in range(MLA_PAIRS):
        x = q[:, nw + p * LANES:nw + (p + 1) * LANES]
        ssq = x * x
        hi = ssq.astype(BF16)
        low = (ssq - hi.astype(F32)).astype(BF16)
        ss = jnp.dot(hi, gmat, preferred_element_type=F32) + jnp.dot(low, gmat, preferred_element_type=F32)
        n = x * lax.rsqrt(ss * (1.0 / MLA_ROPE) + EPS) * gqr_ref[:, p * LANES:(p + 1) * LANES]
        partner = jnp.where(lane < MLA_ROPE, pltpu.roll(n, LANES - MLA_ROPE, axis=1), pltpu.roll(n, MLA_ROPE, axis=1))
        rot = jnp.where(lane < MLA_ROPE, n * cos - partner * sin, n * cos + partner * sin)
        qr = jnp.where(lo, rot * scale, 0.0)
        base = p * MLA_CAT
        qcat_ref[:, base:base + LANES] = qn[:, p * LANES:(p + 1) * LANES].astype(BF16)
        qcat_ref[:, base + LANES:base + MLA_CAT] = qr.astype(BF16)
        kcat_ref[:, base:base + LANES] = kn[:, p * LANES:(p + 1) * LANES].astype(BF16)
        kcat_ref[:, base + LANES:base + MLA_CAT] = kpe
        vp = v[:, p * LANES:(p + 1) * LANES]
        vext_ref[:, base:base + LANES] = jnp.where(lo, vp, 1.0).astype(BF16)
        vext_ref[:, base + LANES:base + MLA_CAT] = jnp.where(lo, 1.0, vp).astype(BF16)


def _mla_up(cqn, ckvn, kpe, wq_p, wkv_p, gqn, gqr, gkn, cos_t, sin_t):
    tm = ROW_TILE
    nw = MLA_NW
    cat = MLA_PAIRS * MLA_CAT
    row = lambda cols: pl.BlockSpec((tm, cols), lambda i: (i, 0))
    full = lambda shape: pl.BlockSpec(shape, lambda i: (0,) * len(shape))
    return pl.pallas_call(
        _mla_up_body,
        grid=(TOKENS // tm,),
        in_specs=[row(MLA_Q_RANK), row(MLA_KV_RANK), row(LANES), full((MLA_Q_RANK, MLA_Q_COLS)),
                  full((MLA_KV_RANK, 2 * nw)), full((1, nw)), full((1, MLA_PAIRS * LANES)),
                  full((1, nw)), row(LANES), row(LANES)],
        out_specs=[row(cat), row(cat), row(cat)],
        out_shape=[jax.ShapeDtypeStruct((TOKENS, cat), BF16)] * 3,
        compiler_params=_params(("arbitrary",)),
        name="mla_up",
    )(cqn, ckvn, kpe, wq_p, wkv_p, gqn, gqr, gkn, cos_t, sin_t)


def _mla_attn_body(q_ref, k_ref, v_ref, o_ref):
    tq, tk = MLA_TQ, MLA_TK
    qi = pl.program_id(2)
    lane = lax.broadcasted_iota(I32, (tq, MLA_CAT), 1)
    rot = lane - LANES
    head0 = (lane < 64) | ((rot >= 0) & (rot < 64) & ((rot % MLA_ROPE) < MLA_HALF))
    head1 = ((lane >= 64) & (lane < LANES)) | ((rot >= 0) & (rot < 64) & ((rot % MLA_ROPE) >= MLA_HALF))
    q = q_ref[...]
    zero = jnp.zeros_like(q)
    q2 = jnp.concatenate([jnp.where(head0, q, zero), jnp.where(head1, q, zero)], axis=0)

    def scores(ki):
        ks = pl.multiple_of(ki * tk, tk)
        return lax.dot_general(q2, k_ref[pl.ds(ks, tk), :], NT_DIMS, preferred_element_type=F32)

    def step(ki, s, carry, masked):
        ks = pl.multiple_of(ki * tk, tk)
        if masked:
            rowg = qi * tq + lax.broadcasted_iota(I32, (tq, tk), 0)
            colg = ki * tk + lax.broadcasted_iota(I32, (tq, tk), 1)
            keep = colg <= rowg
        out = []
        for h in range(2):
            m, acc = carry[h]
            sh = s[h * tq:(h + 1) * tq]
            if masked:
                sh = jnp.where(keep, sh, -jnp.inf)
            m_new = jnp.maximum(m, jnp.max(sh, axis=-1, keepdims=True))
            pb = jnp.exp2(sh - m_new).astype(BF16)
            pv = jnp.dot(pb, v_ref[pl.ds(ks, tk), h * LANES:(h + 1) * LANES], preferred_element_type=F32)
            out.append((m_new, jnp.exp2(m - m_new) * acc + pv))
        return tuple(out)

    init = tuple((jnp.full((tq, 1), -jnp.inf, F32), jnp.zeros((tq, LANES), F32)) for _ in range(2))
    kd = (qi * tq) // tk
    carry = lax.fori_loop(0, kd, lambda ki, c: step(ki, scores(ki), c, False), init)
    (_, acc0), (_, acc1) = step(kd, scores(kd), carry, True)
    half = LANES // 2
    out0 = acc0 / pltpu.roll(acc0, half, axis=1)
    out1 = acc1 / pltpu.roll(acc1, half, axis=1)
    o_ref[...] = jnp.where(lax.broadcasted_iota(I32, (tq, LANES), 1) < half, out0, out1).astype(BF16)


def _mla_attn(qcat, kcat, vext):
    tq = MLA_TQ
    nq = SEQ // tq
    return pl.pallas_call(
        _mla_attn_body,
        grid=(BATCH, MLA_PAIRS, nq),
        in_specs=[pl.BlockSpec((tq, MLA_CAT), lambda b, p, q: (b * nq + q, p)),
                  pl.BlockSpec((SEQ, MLA_CAT), lambda b, p, q: (b, p)),
                  pl.BlockSpec((SEQ, MLA_CAT), lambda b, p, q: (b, p))],
        out_specs=pl.BlockSpec((tq, LANES), lambda b, p, q: (b * nq + q, p)),
        out_shape=jax.ShapeDtypeStruct((TOKENS, MLA_HEADS * MLA_V), BF16),
        compiler_params=_params(("arbitrary", "arbitrary", "arbitrary")),
        name="mla_attn",
    )(qcat, kcat, vext)


def _dil_attn_body(q_ref, k_ref, v_ref, o_ref, os_ref, ls_ref, bias_ref):
    blk = DA_BLOCK
    lo = lax.broadcasted_iota(I32, (blk, LANES), 1) < 64
    rowi = lax.broadcasted_iota(I32, (2 * blk, 2 * blk), 0) % blk
    colj = lax.broadcasted_iota(I32, (2 * blk, 2 * blk), 1)
    rel = rowi - colj
    assert all(window // dil == blk for window, dil in DA_PATTERNS)
    for first, off in ((0, 0), (1, blk)):
        dist = off + rel
        bias_ref[first] = jnp.where((dist >= 0) & (dist <= blk), 0.0, -jnp.inf)

    for pi, (window, dil) in enumerate(DA_PATTERNS):
        n_back = window // dil
        nb = SEQ // dil // blk

        def block(t, carry, pi=pi, dil=dil, n_back=n_back, nb=nb):
            r = t // nb
            n = t % nb
            kb = jnp.maximum(n - 1, 0)
            qs = r + n * (blk * dil)
            ks = r + kb * (blk * dil)
            if dil == 1:
                qs = pl.multiple_of(qs, blk)
                ks = pl.multiple_of(ks, blk)
                q = q_ref[pl.ds(qs, blk), :]
                k2 = k_ref[pl.ds(ks, 2 * blk), :]
                v2 = v_ref[pl.ds(ks, 2 * blk), :]
            else:
                q = q_ref[pl.ds(qs, blk, stride=dil), :]
                k2 = k_ref[pl.ds(ks, 2 * blk, stride=dil), :]
                v2 = v_ref[pl.ds(ks, 2 * blk, stride=dil), :]
            zero = jnp.zeros_like(q)
            q2 = jnp.concatenate([jnp.where(lo, q, zero), jnp.where(lo, zero, q)], axis=0).astype(BF16)
            s = lax.dot_general(q2, k2.astype(BF16), NT_DIMS, preferred_element_type=F32)
            s = s + bias_ref[n - kb]
            m = jnp.max(s, axis=-1, keepdims=True)
            e = jnp.exp(s - m)
            l = jnp.sum(e, axis=-1, keepdims=True)
            pv = jnp.dot((e * (1.0 / l)).astype(BF16), v2.astype(BF16), preferred_element_type=F32)
            lse = m + jnp.log(l)
            o_blk = jnp.where(lo, pv[:blk], pv[blk:])
            l_blk = jnp.where(lo, jnp.broadcast_to(lse[:blk], (blk, LANES)),
                              jnp.broadcast_to(lse[blk:], (blk, LANES)))
            if dil == 1:
                os_ref[pi, pl.ds(qs, blk), :] = o_blk
                ls_ref[pi, pl.ds(qs, blk), :] = l_blk
            else:
                os_ref[pi, pl.ds(qs, blk, stride=dil), :] = o_blk
                ls_ref[pi, pl.ds(qs, blk, stride=dil), :] = l_blk
            return carry

        def group(g, carry, block=block):
            for u in range(DA_INTERLEAVE):
                block(g * DA_INTERLEAVE + u, carry)
            return carry

        lax.fori_loop(0, dil * nb // DA_INTERLEAVE, group, 0)

    chunk = 512

    def combine(c, carry):
        sl = pl.ds(pl.multiple_of(c * chunk, chunk), chunk)
        l0 = ls_ref[0, sl, :]
        l1 = ls_ref[1, sl, :]
        l2 = ls_ref[2, sl, :]
        mx = jnp.maximum(jnp.maximum(l0, l1), l2)
        e0 = jnp.exp(l0 - mx)
        e1 = jnp.exp(l1 - mx)
        e2 = jnp.exp(l2 - mx)
        den = e0 + e1 + e2
        o = (e0 * os_ref[0, sl, :] + e1 * os_ref[1, sl, :] + e2 * os_ref[2, sl, :]) / den
        o_ref[sl, :] = o.astype(BF16)
        return carry

    lax.fori_loop(0, SEQ // chunk, combine, 0)


def _dil_attn(qa, ka, va):
    pairs = DA_HEADS // 2
    spec = pl.BlockSpec((SEQ, LANES), lambda b, p: (b, p))
    n_pat = len(DA_PATTERNS)
    return pl.pallas_call(
        _dil_attn_body,
        grid=(BATCH, pairs),
        in_specs=[spec, spec, spec],
        out_specs=spec,
        out_shape=jax.ShapeDtypeStruct((TOKENS, DA_WIDTH), BF16),
        scratch_shapes=[pltpu.VMEM((n_pat, SEQ, LANES), F32), pltpu.VMEM((n_pat, SEQ, LANES), F32),
                        pltpu.VMEM((2, 2 * DA_BLOCK, 2 * DA_BLOCK), F32)],
        compiler_params=_params(("arbitrary", "arbitrary")),
        name="dil_attn",
    )(qa, ka, va)


def _out_proj_body(*refs, n_in, has_bias):
    a_refs = refs[:n_in]
    w_refs = refs[n_in:2 * n_in]
    rest = refs[2 * n_in:]
    if has_bias:
        b_ref, x_ref, g_ref, o_ref = rest
    else:
        x_ref, g_ref, o_ref = rest
    y = jnp.dot(a_refs[0][...], w_refs[0][...], preferred_element_type=F32)
    for a_ref, w_ref in zip(a_refs[1:], w_refs[1:]):
        y = y + jnp.dot(a_ref[...], w_ref[...], preferred_element_type=F32)
    if has_bias:
        y = y + b_ref[...]
    o_ref[...] = x_ref[...] + g_ref[0] * y


def _out_proj(acts, weights, bias, x, mods):
    tm = ROW_TILE
    tpb = SEQ // tm
    in_specs = [pl.BlockSpec((tm, a.shape[1]), lambda i: (i, 0)) for a in acts]
    in_specs += [pl.BlockSpec(w.shape, lambda i: (0, 0)) for w in weights]
    args = list(acts) + list(weights)
    if bias is not None:
        in_specs.append(pl.BlockSpec((1, D_MODEL), lambda i: (0, 0)))
        args.append(bias)
    in_specs += [pl.BlockSpec((tm, D_MODEL), lambda i: (i, 0)), _mod_spec(2, tpb)]
    args += [x, mods]
    return pl.pallas_call(
        functools.partial(_out_proj_body, n_in=len(acts), has_bias=bias is not None),
        grid=(TOKENS // tm,),
        in_specs=in_specs,
        out_specs=pl.BlockSpec((tm, D_MODEL), lambda i: (i, 0)),
        out_shape=jax.ShapeDtypeStruct((TOKENS, D_MODEL), F32),
        compiler_params=_params(("arbitrary",)),
        name="out_proj",
    )(*args)


def _in_rec_body(x_ref, sh_ref, sc_ref, g_ref, w_ref, b_ref, gate_ref, xr_ref):
    h = _prenorm(x_ref[...], g_ref[...], sh_ref[0], sc_ref[0])
    gx = jnp.dot(h.astype(BF16), w_ref[...], preferred_element_type=F32) + b_ref[...]
    gate_ref[...] = jax.nn.gelu(gx[:, :LRU_WIDTH], approximate=True)
    xr_ref[...] = gx[:, LRU_WIDTH:]


def _in_rec(x, mods, gain, w, b):
    tm = ROW_TILE
    tpb = SEQ // tm
    row = lambda cols: pl.BlockSpec((tm, cols), lambda i: (i, 0))
    full = lambda shape: pl.BlockSpec(shape, lambda i: (0,) * len(shape))
    return pl.pallas_call(
        _in_rec_body,
        grid=(TOKENS // tm,),
        in_specs=[row(D_MODEL), _mod_spec(0, tpb), _mod_spec(1, tpb), full((1, D_MODEL)),
                  full((D_MODEL, 2 * LRU_WIDTH)), full((1, 2 * LRU_WIDTH))],
        out_specs=[row(LRU_WIDTH), row(LRU_WIDTH)],
        out_shape=[jax.ShapeDtypeStruct((TOKENS, LRU_WIDTH), F32)] * 2,
        compiler_params=_params(("arbitrary",)),
        name="in_rec",
    )(x, mods, mods, gain, w, b)


def _rglru_body(xr_ref, g_ref, cw_ref, cb_ref, wa_ref, ba_ref, wx_ref, bx_ref, lam_ref, o_ref,
                tail_ref, h_ref):
    ts = SCAN_ROWS
    width = LRU_WIDTH

    @pl.when(pl.program_id(1) == 0)
    def _():
        tail_ref[...] = jnp.zeros_like(tail_ref)
        h_ref[...] = jnp.zeros_like(h_ref)

    xr = xr_ref[...]
    tail = tail_ref[...]
    row8 = lax.broadcasted_iota(I32, (SUBLANES, width), 0)
    cw = cw_ref[...]
    xc = xr * cw[CONV_WIDTH - 1:CONV_WIDTH, :] + cb_ref[...]
    for s in range(1, CONV_WIDTH):
        rolled = pltpu.roll(xr, s, axis=0)
        fix = pltpu.roll(tail, s, axis=0)
        first = jnp.where(row8 < s, fix, rolled[:SUBLANES])
        shifted = jnp.concatenate([first, rolled[SUBLANES:]], axis=0)
        xc = xc + shifted * cw[CONV_WIDTH - 1 - s:CONV_WIDTH - s, :]
    tail_ref[...] = xr[ts - SUBLANES:, :]

    xcb = xc.astype(BF16)
    bd = LRU_BLOCK_DIM
    ra = jnp.concatenate([jnp.dot(xcb[:, n * bd:(n + 1) * bd], wa_ref[n], preferred_element_type=F32)
                          for n in range(LRU_BLOCKS)], axis=1) + ba_ref[...]
    rx = jnp.concatenate([jnp.dot(xcb[:, n * bd:(n + 1) * bd], wx_ref[n], preferred_element_type=F32)
                          for n in range(LRU_BLOCKS)], axis=1) + bx_ref[...]
    r = jax.nn.sigmoid(ra)
    ig = jax.nn.sigmoid(rx)
    z = -lam_ref[...]
    softplus = jnp.maximum(z, 0.0) + jnp.log1p(jnp.exp(-jnp.abs(z)))
    log_a = (-LRU_C) * r * softplus
    a = jnp.exp(log_a)
    bt = jnp.sqrt(-jnp.tanh(log_a) * (a * a + 1.0)) * (ig * xc)

    rowi = lax.broadcasted_iota(I32, (ts, LANES), 0)
    hs = []
    for c in range(width // LANES):
        ac = a[:, c * LANES:(c + 1) * LANES]
        bc = bt[:, c * LANES:(c + 1) * LANES]
        k = 1
        while k < ts:
            keep = rowi >= k
            a_sh = jnp.where(keep, pltpu.roll(ac, k, axis=0), 1.0)
            b_sh = jnp.where(keep, pltpu.roll(bc, k, axis=0), 0.0)
            bc = ac * b_sh + bc
            ac = ac * a_sh
            k *= 2
        hs.append(ac * h_ref[0:1, c * LANES:(c + 1) * LANES] + bc)
    h = jnp.concatenate(hs, axis=1)
    h_ref[0:1, :] = h[ts - 1:ts, :]
    o_ref[...] = (g_ref[...] * h).astype(BF16)


def _rglru(xr, gate, cw, cb, wa, ba, wx, bx, lam):
    ts = SCAN_ROWS
    nt = SEQ // ts
    row = pl.BlockSpec((ts, LRU_WIDTH), lambda b, j: (b * nt + j, 0))
    full = lambda shape: pl.BlockSpec(shape, lambda b, j: (0,) * len(shape))
    vec = full((1, LRU_WIDTH))
    wspec = full((LRU_BLOCKS, LRU_BLOCK_DIM, LRU_BLOCK_DIM))
    return pl.pallas_call(
        _rglru_body,
        grid=(BATCH, nt),
        in_specs=[row, row, full((CONV_WIDTH, LRU_WIDTH)), vec, wspec, vec, wspec, vec, vec],
        out_specs=row,
        out_shape=jax.ShapeDtypeStruct((TOKENS, LRU_WIDTH), BF16),
        scratch_shapes=[pltpu.VMEM((SUBLANES, LRU_WIDTH), F32), pltpu.VMEM((SUBLANES, LRU_WIDTH), F32)],
        compiler_params=_params(("arbitrary", "arbitrary")),
        name="rglru",
    )(xr, gate, cw, cb, wa, ba, wx, bx, lam)


def _router_body(x_ref, sh_ref, sc_ref, g_ref, wr_ref, br_ref,
                 hn_ref, idx_ref, rank_ref, gate_ref, cnt_ref, carry_ref):
    tm = ROW_TILE

    @pl.when(pl.program_id(0) == 0)
    def _():
        carry_ref[...] = jnp.zeros_like(carry_ref)

    h = _prenorm(x_ref[...], g_ref[...], sh_ref[0], sc_ref[0])
    _store_row_tiles(hn_ref, h)
    hb = h.astype(BF16)
    hl = (h - hb.astype(F32)).astype(BF16)
    wr = wr_ref[...]
    wh = wr.astype(BF16)
    wl = (wr - wh.astype(F32)).astype(BF16)
    logits = (jnp.dot(hb, wh, preferred_element_type=F32) + jnp.dot(hb, wl, preferred_element_type=F32)
              + jnp.dot(hl, wh, preferred_element_type=F32)) + br_ref[...]
    lane = lax.broadcasted_iota(I32, (tm, LANES), 1)
    lanef = lane.astype(F32)
    neg = -jnp.inf
    work = logits
    tops, hots = [], []
    for _ in range(TOP_K):
        m = jnp.max(work, axis=-1, keepdims=True)
        first = jnp.min(jnp.where(work == m, lanef, float(LANES)), axis=-1, keepdims=True)
        hot = lanef == first
        work = jnp.where(hot, neg, work)
        tops.append((m, first))
        hots.append(hot)
    exps = [jnp.exp(m - tops[0][0]) for m, _ in tops]
    den = exps[0] + exps[1] + exps[2] + exps[3]
    sel = jnp.zeros((tm, LANES), F32)
    for hot in hots:
        sel = jnp.where(hot, 1.0, sel)
    ri = lax.broadcasted_iota(I32, (tm, tm), 0)
    ci = lax.broadcasted_iota(I32, (tm, tm), 1)
    tril = jnp.where(ci < ri, 1.0, 0.0).astype(BF16)
    cum = jnp.dot(tril, sel.astype(BF16), preferred_element_type=F32) + carry_ref[0:1, :]
    idx_slab = jnp.zeros((tm, LANES), I32)
    rank_slab = jnp.zeros((tm, LANES), I32)
    gate_slab = jnp.zeros((tm, LANES), F32)
    for k in range(TOP_K):
        rank = jnp.sum(jnp.where(hots[k], cum, 0.0), axis=-1, keepdims=True)
        idx_slab = jnp.where(lane == k, tops[k][1].astype(I32), idx_slab)
        rank_slab = jnp.where(lane == k, rank.astype(I32), rank_slab)
        gate_slab = jnp.where(lane == k, exps[k] / den, gate_slab)
    idx_ref[...] = idx_slab
    rank_ref[...] = rank_slab
    gate_ref[...] = gate_slab
    total = carry_ref[0:1, :] + jnp.sum(sel, axis=0, keepdims=True)
    carry_ref[0:1, :] = total
    cnt_ref[...] = total


def _router(x, mods, gain, wr_p, br_p):
    tm = ROW_TILE
    tpb = SEQ // tm
    row = lambda cols: pl.BlockSpec((tm, cols), lambda i: (i, 0))
    full = lambda shape: pl.BlockSpec(shape, lambda i: (0,) * len(shape))
    return pl.pallas_call(
        _router_body,
        grid=(TOKENS // tm,),
        in_specs=[row(D_MODEL), _mod_spec(3, tpb), _mod_spec(4, tpb), full((1, D_MODEL)),
                  full((D_MODEL, LANES)), full((1, LANES))],
        out_specs=[pl.BlockSpec((tm * ROW_CHUNKS, LANES), lambda i: (i, 0)),
                   row(LANES), row(LANES), row(LANES), full((1, LANES))],
        out_shape=[jax.ShapeDtypeStruct((TOKENS * ROW_CHUNKS, LANES), F32),
                   jax.ShapeDtypeStruct((TOKENS, LANES), I32),
                   jax.ShapeDtypeStruct((TOKENS, LANES), I32),
                   jax.ShapeDtypeStruct((TOKENS, LANES), F32),
                   jax.ShapeDtypeStruct((1, LANES), F32)],
        scratch_shapes=[pltpu.VMEM((SUBLANES, LANES), F32)],
        compiler_params=_params(("arbitrary",)),
        name="router",
    )(x, mods, mods, gain, wr_p, br_p)


def _tile_copy(src, src_row, dst, dst_row, sem):
    s0 = pl.multiple_of(src_row * ROW_CHUNKS, ROW_CHUNKS)
    d0 = pl.multiple_of(dst_row * ROW_CHUNKS, ROW_CHUNKS)
    return pltpu.make_async_copy(src.at[pl.ds(s0, ROW_CHUNKS), :], dst.at[pl.ds(d0, ROW_CHUNKS), :], sem)


def _store_row_tiles(ref, val):
    rows = val.shape[0]
    for c in range(ROW_CHUNKS):
        ref[pl.ds(c, rows, stride=ROW_CHUNKS), :] = val[:, c * LANES:(c + 1) * LANES]


def _load_row_tiles(ref, rows):
    return jnp.concatenate([ref[pl.ds(c, rows, stride=ROW_CHUNKS), :] for c in range(ROW_CHUNKS)], axis=1)


def _moe_dispatch_body(padbase_ref, npad_ref, nused_ref, dest_ref, hn_ref, xs_hbm, zero_ref, sem, *, n_blocks):
    td = ROW_TILE
    block_rows = MOE_ROWS * ROW_CHUNKS

    @pl.when(pl.program_id(0) == 0)
    def _():
        zero_ref[...] = jnp.zeros_like(zero_ref)

        def per_expert(e, carry, wait):
            def per_row(r, c):
                cp = _tile_copy(zero_ref, 0, xs_hbm, padbase_ref[e] + r, sem)
                if wait:
                    cp.wait()
                else:
                    cp.start()
                return c

            return lax.fori_loop(0, npad_ref[e], per_row, carry)

        def per_block(b, carry, wait):
            start = pl.multiple_of(b * block_rows, block_rows)
            cp = pltpu.make_async_copy(zero_ref, xs_hbm.at[pl.ds(start, block_rows), :], sem)
            if wait:
                cp.wait()
            else:
                cp.start()
            return carry

        for wait in (False, True):
            lax.fori_loop(0, N_EXPERTS, functools.partial(per_expert, wait=wait), 0)
            lax.fori_loop(nused_ref[0], n_blocks, functools.partial(per_block, wait=wait), 0)

    def issue(r, carry):
        for k in range(TOP_K):
            _tile_copy(hn_ref, r, xs_hbm, dest_ref[0, 0, r * TOP_K + k], sem).start(priority=k % 2)
        return carry

    lax.fori_loop(0, td, issue, 0, unroll=8)

    def drain(r, carry):
        for k in range(TOP_K):
            _tile_copy(hn_ref, r, xs_hbm, 0, sem).wait()
        return carry

    lax.fori_loop(0, td, drain, 0, unroll=8)


def _moe_dispatch(pad_base, n_pad, n_used, dest, hn_tiles, n_blocks):
    td = ROW_TILE
    nt = TOKENS // td
    grid_spec = pltpu.PrefetchScalarGridSpec(
        num_scalar_prefetch=3,
        grid=(nt,),
        in_specs=[pl.BlockSpec((1, 1, td * TOP_K), lambda i, *_: (i, 0, 0), memory_space=pltpu.SMEM),
                  pl.BlockSpec((td * ROW_CHUNKS, LANES), lambda i, *_: (i, 0))],
        out_specs=pl.BlockSpec(memory_space=pl.ANY),
        scratch_shapes=[pltpu.VMEM((MOE_ROWS * ROW_CHUNKS, LANES), F32), pltpu.SemaphoreType.DMA],
    )
    return pl.pallas_call(
        functools.partial(_moe_dispatch_body, n_blocks=n_blocks),
        grid_spec=grid_spec,
        out_shape=jax.ShapeDtypeStruct((n_blocks * MOE_ROWS * ROW_CHUNKS, LANES), F32),
        compiler_params=_params(("arbitrary",)),
        name="moe_dispatch",
    )(pad_base, n_pad, n_used, dest.reshape(nt, 1, td * TOP_K), hn_tiles)


def _moe_ffn_body(be_ref, nused_ref, xs_ref, w1_ref, b1_ref, w2_ref, b2_ref, ys_ref, ht_ref):
    bm = MOE_ROWS
    i = pl.program_id(0)

    @pl.when(i < nused_ref[0])
    def _():
        h = jnp.dot(_load_row_tiles(xs_ref, bm), w1_ref[0], preferred_element_type=F32) + b1_ref[0]
        ht = h.T
        for j in range(bm // LANES):
            ht_ref[j] = ht[:, j * LANES:(j + 1) * LANES]
        parts = []
        for j in range(bm // LANES):
            gate = jnp.minimum(ht_ref[j, pl.ds(0, D_EXPERT, stride=2), :], SWIGLU_LIMIT)
            lin = jnp.clip(ht_ref[j, pl.ds(1, D_EXPERT, stride=2), :], -SWIGLU_LIMIT, SWIGLU_LIMIT)
            parts.append(gate * jax.nn.sigmoid(SWIGLU_ALPHA * gate) * (lin + 1.0))
        act_t = jnp.concatenate(parts, axis=1)
        y = lax.dot_general(act_t, w2_ref[0], (((0,), (0,)), ((), ())), preferred_element_type=F32)
        _store_row_tiles(ys_ref, y + b2_ref[0])

    @pl.when(i >= nused_ref[0])
    def _():
        ys_ref[...] = jnp.zeros_like(ys_ref)


def _moe_ffn(block_expert, n_used, xs, w1, b1, w2, b2, layer, n_blocks):
    bm = MOE_ROWS
    f2 = 2 * D_EXPERT
    base = layer * N_EXPERTS
    grid_spec = pltpu.PrefetchScalarGridSpec(
        num_scalar_prefetch=2,
        grid=(n_blocks,),
        in_specs=[pl.BlockSpec((bm * ROW_CHUNKS, LANES), lambda i, be, nu: (jnp.minimum(i, nu[0] - 1), 0)),
                  pl.BlockSpec((1, D_MODEL, f2), lambda i, be, nu: (base + be[i], 0, 0)),
                  pl.BlockSpec((1, 1, f2), lambda i, be, nu: (base + be[i], 0, 0)),
                  pl.BlockSpec((1, D_EXPERT, D_MODEL), lambda i, be, nu: (base + be[i], 0, 0)),
                  pl.BlockSpec((1, 1, D_MODEL), lambda i, be, nu: (base + be[i], 0, 0))],
        out_specs=pl.BlockSpec((bm * ROW_CHUNKS, LANES), lambda i, be, nu: (i, 0)),
        scratch_shapes=[pltpu.VMEM((bm // LANES, f2, LANES), F32)],
    )
    return pl.pallas_call(
        _moe_ffn_body,
        grid_spec=grid_spec,
        out_shape=jax.ShapeDtypeStruct((n_blocks * bm * ROW_CHUNKS, LANES), F32),
        compiler_params=_params(("arbitrary",)),
        name="moe_ffn",
    )(block_expert, n_used, xs,
      w1.reshape(DEPTH * N_EXPERTS, D_MODEL, f2), b1.reshape(DEPTH * N_EXPERTS, 1, f2),
      w2.reshape(DEPTH * N_EXPERTS, D_EXPERT, D_MODEL), b2.reshape(DEPTH * N_EXPERTS, 1, D_MODEL))


def _moe_combine_body(dest_ref, dnext_ref, gates_ref, x_ref, g_ref, ys_hbm, o_ref, buf0_ref, buf1_ref, sem):
    tc = COMBINE_ROWS
    i = pl.program_id(0)
    bufs = (buf0_ref, buf1_ref)

    def issue(d_ref, slot):
        def body(r, carry):
            for k in range(TOP_K):
                _tile_copy(ys_hbm, d_ref[0, 0, r * TOP_K + k], bufs[slot].at[k], r,
                           sem.at[slot]).start(priority=k % 2)
            return carry

        lax.fori_loop(0, tc, body, 0, unroll=8)

    def drain(slot):
        def body(r, carry):
            for k in range(TOP_K):
                _tile_copy(ys_hbm, 0, bufs[slot].at[k], r, sem.at[slot]).wait()
            return carry

        lax.fori_loop(0, tc, body, 0, unroll=8)

    def finish(slot):
        drain(slot)
        gates = gates_ref[...]
        gate = g_ref[0]
        for c in range(ROW_CHUNKS):
            rows = pl.ds(c, tc, stride=ROW_CHUNKS)
            y = gates[:, 0:1] * bufs[slot][0, rows, :]
            for k in range(1, TOP_K):
                y = y + gates[:, k:k + 1] * bufs[slot][k, rows, :]
            cols = slice(c * LANES, (c + 1) * LANES)
            o_ref[:, cols] = x_ref[:, cols] + gate[:, cols] * y

    @pl.when(i == 0)
    def _():
        issue(dest_ref, 0)

    for slot in range(2):
        @pl.when(i % 2 == slot)
        def _(slot=slot):
            @pl.when(i + 1 < pl.num_programs(0))
            def _():
                issue(dnext_ref, 1 - slot)

            finish(slot)


def _moe_combine(dest, gates, x, mods, ys):
    tc = COMBINE_ROWS
    tpb = SEQ // tc
    nt = TOKENS // tc
    dest3 = dest.reshape(nt, 1, tc * TOP_K)
    buf = pltpu.VMEM((TOP_K, tc * ROW_CHUNKS, LANES), F32)
    return pl.pallas_call(
        _moe_combine_body,
        grid=(nt,),
        in_specs=[pl.BlockSpec((1, 1, tc * TOP_K), lambda i: (i, 0, 0), memory_space=pltpu.SMEM),
                  pl.BlockSpec((1, 1, tc * TOP_K), lambda i: (jnp.minimum(i + 1, nt - 1), 0, 0),
                               memory_space=pltpu.SMEM),
                  pl.BlockSpec((tc, LANES), lambda i: (i, 0)),
                  pl.BlockSpec((tc, D_MODEL), lambda i: (i, 0)),
                  _mod_spec(5, tpb),
                  pl.BlockSpec(memory_space=pl.ANY)],
        out_specs=pl.BlockSpec((tc, D_MODEL), lambda i: (i, 0)),
        out_shape=jax.ShapeDtypeStruct((TOKENS, D_MODEL), F32),
        scratch_shapes=[buf, buf, pltpu.SemaphoreType.DMA((2,))],
        compiler_params=_params(("arbitrary",)),
        name="moe_combine",
    )(dest3, dest3, gates, x, mods, ys)


def _attn_layer(x, mods, gain, j, cos_t, sin_t, p):
    w_in = p["attn_w_in"][j]
    rope_cols = A_COLS + MLA_Q_RANK + MLA_KV_RANK
    w_p = jnp.concatenate(
        [w_in[:, :rope_cols],
         jnp.tile(w_in[:, rope_cols:rope_cols + MLA_HALF], (1, MLA_HEADS)),
         jnp.tile(w_in[:, rope_cols + MLA_HALF:], (1, MLA_HEADS))], axis=1).astype(BF16)
    tile_h = lambda v: jnp.tile(v, MLA_HEADS).reshape(1, -1)
    kg = p["mla_k_gain"][j]
    qg = p["mla_q_gain"][j]
    qa, ka, va, cqn, ckvn, kpe = _in_attn(
        x, mods, gain, w_p, tile_h(p["da_q_gain"][j]), tile_h(p["da_k_gain"][j]),
        p["mla_qa_gain"][j].reshape(1, -1), p["mla_kva_gain"][j].reshape(1, -1),
        tile_h(kg[MLA_NOPE:MLA_NOPE + MLA_HALF]), tile_h(kg[MLA_NOPE + MLA_HALF:]), cos_t, sin_t)

    wq = p["mla_w_uq"][j].reshape(MLA_Q_RANK, MLA_HEADS, MLA_QK)
    first = slice(MLA_NOPE, MLA_NOPE + MLA_HALF)
    second = slice(MLA_NOPE + MLA_HALF, MLA_QK)
    pad_cols = jnp.zeros((MLA_Q_RANK, LANES - 2 * MLA_ROPE), F32)
    rot_blocks = [jnp.concatenate([wq[:, 2 * q, first], wq[:, 2 * q + 1, first],
                                   wq[:, 2 * q, second], wq[:, 2 * q + 1, second], pad_cols], axis=1)
                  for q in range(MLA_PAIRS)]
    wq_p = jnp.concatenate([wq[:, :, :MLA_NOPE].reshape(MLA_Q_RANK, -1)] + rot_blocks, axis=1).astype(BF16)
    gqr = jnp.tile(jnp.concatenate([qg[first], qg[first], qg[second], qg[second],
                                    jnp.zeros((LANES - 2 * MLA_ROPE,), F32)]), MLA_PAIRS).reshape(1, -1)
    wkv = p["mla_w_ukv"][j].reshape(MLA_KV_RANK, MLA_HEADS, MLA_NOPE + MLA_V)
    wkv_p = jnp.concatenate(
        [wkv[:, :, :MLA_NOPE].reshape(MLA_KV_RANK, -1),
         wkv[:, :, MLA_NOPE:].reshape(MLA_KV_RANK, -1)], axis=1).astype(BF16)
    qcat, kcat, vext = _mla_up(cqn, ckvn, kpe, wq_p, wkv_p, tile_h(qg[:MLA_NOPE]), gqr,
                               tile_h(kg[:MLA_NOPE]), cos_t, sin_t)

    o_a = _dil_attn(qa, ka, va)
    o_b = _mla_attn(qcat, kcat, vext)
    w_out = p["attn_w_out"][j].astype(BF16)
    return _out_proj([o_a, o_b], [w_out[:DA_WIDTH], w_out[DA_WIDTH:]], None, x, mods)


def _rec_layer(x, mods, gain, j, p):
    gate, xr = _in_rec(x, mods, gain, p["rec_w_in"][j].astype(BF16), p["rec_b_in"][j].reshape(1, -1))
    gy = _rglru(xr, gate, p["rec_conv_w"][j], p["rec_conv_b"][j].reshape(1, -1),
                p["rec_wa"][j].astype(BF16), p["rec_ba"][j].reshape(1, -1),
                p["rec_wx"][j].astype(BF16), p["rec_bx"][j].reshape(1, -1),
                p["rec_lambda"][j].reshape(1, -1))
    return _out_proj([gy], [p["rec_w_out"][j].astype(BF16)], p["rec_b_out"][j].reshape(1, -1), x, mods)


def _moe_layer(x, mods, gain, layer, p):
    bm = MOE_ROWS
    wr_p = jnp.pad(p["moe_w_router"][layer], ((0, 0), (0, LANES - N_EXPERTS)))
    br_p = jnp.pad(p["moe_b_router"][layer], (0, LANES - N_EXPERTS), constant_values=-jnp.inf).reshape(1, LANES)
    hn, idx_slab, rank_slab, gate_slab, cnt = _router(x, mods, gain, wr_p, br_p)

    counts = cnt[0, :N_EXPERTS].astype(I32)
    padded = (counts + bm - 1) // bm * bm
    pad_ends = jnp.cumsum(padded)
    pad_starts = pad_ends - padded
    n_blocks = TOKENS * TOP_K // bm + N_EXPERTS
    experts = jnp.arange(N_EXPERTS, dtype=I32)
    idx = idx_slab[:, :TOP_K]
    dest = rank_slab[:, :TOP_K] + jnp.sum(
        jnp.where(idx[:, :, None] == experts, pad_starts, 0), axis=-1, dtype=I32)
    block_start = jnp.arange(n_blocks, dtype=I32) * bm
    block_expert = jnp.minimum(
        jnp.sum(pad_ends[None, :] <= block_start[:, None], axis=-1, dtype=I32), N_EXPERTS - 1)
    n_used = (pad_ends[-1:] // bm).astype(I32)

    xs = _moe_dispatch(pad_starts + counts, padded - counts, n_used, dest, hn, n_blocks)
    ys = _moe_ffn(block_expert, n_used, xs, p["moe_w1"], p["moe_b1"], p["moe_w2"], p["moe_b2"], layer, n_blocks)
    return _moe_combine(dest, gate_slab, x, mods, ys)


def kernel(x, c, positions, ada_w, ada_b, norm1_gain, norm2_gain, attn_w_in, da_q_gain, da_k_gain, mla_qa_gain, mla_kva_gain, mla_w_uq, mla_w_ukv, mla_q_gain, mla_k_gain, attn_w_out, rec_w_in, rec_b_in, rec_conv_w, rec_conv_b, rec_wa, rec_ba, rec_wx, rec_bx, rec_lambda, rec_w_out, rec_b_out, moe_w_router, moe_b_router, moe_w1, moe_b1, moe_w2, moe_b2):
    p = dict(attn_w_in=attn_w_in, da_q_gain=da_q_gain, da_k_gain=da_k_gain, mla_qa_gain=mla_qa_gain,
             mla_kva_gain=mla_kva_gain, mla_w_uq=mla_w_uq, mla_w_ukv=mla_w_ukv, mla_q_gain=mla_q_gain,
             mla_k_gain=mla_k_gain, attn_w_out=attn_w_out, rec_w_in=rec_w_in, rec_b_in=rec_b_in,
             rec_conv_w=rec_conv_w, rec_conv_b=rec_conv_b, rec_wa=rec_wa, rec_ba=rec_ba, rec_wx=rec_wx,
             rec_bx=rec_bx, rec_lambda=rec_lambda, rec_w_out=rec_w_out, rec_b_out=rec_b_out,
             moe_w_router=moe_w_router, moe_b_router=moe_b_router, moe_w1=moe_w1, moe_b1=moe_b1,
             moe_w2=moe_w2, moe_b2=moe_b2)
    mods_all = _ada(c, ada_w, ada_b)
    cos_t, sin_t = _rope_tables(positions)
    xt = x.reshape(TOKENS, D_MODEL)
    for layer in range(DEPTH):
        mods = mods_all[layer]
        j = layer // 2
        gain1 = norm1_gain[layer].reshape(1, D_MODEL)
        if layer % 2 == 0:
            xt = _attn_layer(xt, mods, gain1, j, cos_t, sin_t, p)
        else:
            xt = _rec_layer(xt, mods, gain1, j, p)
        xt = _moe_layer(xt, mods, norm2_gain[layer].reshape(1, D_MODEL), layer, p)
    return xt.reshape(BATCH, SEQ, D_MODEL)
```

```python
import functools

import jax
import jax.numpy as jnp
from jax import lax
from jax.experimental import pallas as pl
from jax.experimental.pallas import tpu as pltpu

F32 = jnp.float32
BF16 = jnp.bfloat16
I32 = jnp.int32

D_MODEL = 1024
BATCH = 4
SEQ = 4096
TOKENS = BATCH * SEQ
DEPTH = 4
EPS = 1e-6

DA_HEADS = 8
DA_HEAD_DIM = 64
DA_PATTERNS = ((128, 1), (512, 4), (2048, 16))
DA_BLOCK = 128
DA_WIDTH = DA_HEADS * DA_HEAD_DIM
DA_INTERLEAVE = 16

MLA_HEADS = 8
MLA_Q_RANK = 256
MLA_KV_RANK = 128
MLA_NOPE = 64
MLA_ROPE = 32
MLA_HALF = MLA_ROPE // 2
MLA_V = 64
MLA_QK = MLA_NOPE + MLA_ROPE
ROPE_THETA = 10000.0
A_COLS = 3 * DA_WIDTH

LRU_WIDTH = D_MODEL
LRU_BLOCKS = 8
LRU_BLOCK_DIM = LRU_WIDTH // LRU_BLOCKS
CONV_WIDTH = 4
LRU_C = 8.0

N_EXPERTS = 32
TOP_K = 4
D_EXPERT = D_MODEL
SWIGLU_LIMIT = 7.0
SWIGLU_ALPHA = 1.702

LANES = 128
SUBLANES = 8
ROW_CHUNKS = D_MODEL // LANES
assert ROW_CHUNKS == SUBLANES
VMEM_LIMIT = 56 * 1024 * 1024

ROW_TILE = 512
MOE_ROWS = 512
COMBINE_ROWS = 256
SCAN_ROWS = 128
MLA_TQ = 1024
MLA_TK = 1024
assert MLA_TK % MLA_TQ == 0

NT_DIMS = (((1,), (1,)), ((), ()))


def _params(sem, vmem=VMEM_LIMIT):
    return pltpu.CompilerParams(dimension_semantics=sem, vmem_limit_bytes=vmem)


def _prenorm(x, gain, shift, scale):
    ms = jnp.mean(x * x, axis=-1, keepdims=True)
    return (x * lax.rsqrt(ms + EPS) * gain) * (1.0 + scale) + shift


def _group_norm64(y):
    rows = y.shape[0]
    lo = lax.broadcasted_iota(I32, (rows, LANES), 1) < 64
    outs = []
    for j in range(y.shape[1] // LANES):
        c = y[:, j * LANES:(j + 1) * LANES]
        sq = c * c
        s0 = jnp.sum(jnp.where(lo, sq, 0.0), axis=-1, keepdims=True)
        s1 = jnp.sum(jnp.where(lo, 0.0, sq), axis=-1, keepdims=True)
        r = jnp.where(lo, lax.rsqrt(s0 * (1.0 / 64) + EPS), lax.rsqrt(s1 * (1.0 / 64) + EPS))
        outs.append(c * r)
    return jnp.concatenate(outs, axis=-1)


def _ada_body(c_ref, w_ref, b_ref, o_ref):
    c = c_ref[...]
    ca = c * jax.nn.sigmoid(c)
    o_ref[0] = jnp.dot(ca.astype(BF16), w_ref[0].astype(BF16), preferred_element_type=F32) + b_ref[0]


def _ada(c, ada_w, ada_b):
    c8 = jnp.pad(c, ((0, SUBLANES - BATCH), (0, 0)))
    out = pl.pallas_call(
        _ada_body,
        grid=(DEPTH, 6),
        in_specs=[pl.BlockSpec((SUBLANES, D_MODEL), lambda l, j: (0, 0)),
                  pl.BlockSpec((1, D_MODEL, D_MODEL), lambda l, j: (l, 0, j)),
                  pl.BlockSpec((1, 1, D_MODEL), lambda l, j: (l, 0, j))],
        out_specs=pl.BlockSpec((1, SUBLANES, D_MODEL), lambda l, j: (l, 0, j)),
        out_shape=jax.ShapeDtypeStruct((DEPTH, SUBLANES, 6 * D_MODEL), F32),
        compiler_params=_params(("arbitrary", "arbitrary")),
        name="ada",
    )(c8, ada_w, ada_b.reshape(DEPTH, 1, 6 * D_MODEL))
    return out[:, :BATCH].reshape(DEPTH, BATCH * 6, 1, D_MODEL)


def _mod_spec(j, rows_per_batch_tiles):
    return pl.BlockSpec((1, 1, D_MODEL), lambda i: ((i // rows_per_batch_tiles) * 6 + j, 0, 0))


def _rope_body(pos_ref, inv_ref, cos_ref, sin_ref):
    ang = pos_ref[...].astype(F32) * inv_ref[...]
    cos_ref[...] = jnp.cos(ang)
    sin_ref[...] = jnp.sin(ang)


def _rope_tables(positions):
    inv = ROPE_THETA ** (-jnp.arange(0, MLA_ROPE, 2, dtype=F32) / MLA_ROPE)
    inv_t = jnp.tile(inv, MLA_HEADS).reshape(1, LANES)
    tm = 512
    return pl.pallas_call(
        _rope_body,
        grid=(TOKENS // tm,),
        in_specs=[pl.BlockSpec((tm, 1), lambda i: (i, 0)),
                  pl.BlockSpec((1, LANES), lambda i: (0, 0))],
        out_specs=[pl.BlockSpec((tm, LANES), lambda i: (i, 0))] * 2,
        out_shape=[jax.ShapeDtypeStruct((TOKENS, LANES), F32)] * 2,
        compiler_params=_params(("arbitrary",)),
        name="rope_tables",
    )(positions.reshape(TOKENS, 1), inv_t)


IN_ATTN_COLS = A_COLS + MLA_Q_RANK + MLA_KV_RANK + 2 * LANES


def _in_attn_body(x_ref, sh_ref, sc_ref, g_ref, w_ref, gq_ref, gk_ref, gcq_ref, gckv_ref,
                  gk1_ref, gk2_ref, cos_ref, sin_ref,
                  qa_ref, ka_ref, va_ref, cq_ref, ckv_ref, kpe_ref):
    h = _prenorm(x_ref[...], g_ref[...], sh_ref[0], sc_ref[0])
    proj = jnp.dot(h.astype(BF16), w_ref[...], preferred_element_type=F32)
    w = DA_WIDTH
    qa_ref[...] = _group_norm64(proj[:, 0:w]) * gq_ref[...] * (DA_HEAD_DIM ** -0.5)
    ka_ref[...] = _group_norm64(proj[:, w:2 * w]) * gk_ref[...]
    va_ref[...] = proj[:, 2 * w:3 * w]
    o = A_COLS
    cq = proj[:, o:o + MLA_Q_RANK]
    cq_ref[...] = (cq * lax.rsqrt(jnp.mean(cq * cq, axis=-1, keepdims=True) + EPS) * gcq_ref[...]).astype(BF16)
    o += MLA_Q_RANK
    ckv = proj[:, o:o + MLA_KV_RANK]
    ckv_ref[...] = (ckv * lax.rsqrt(jnp.mean(ckv * ckv, axis=-1, keepdims=True) + EPS) * gckv_ref[...]).astype(BF16)
    o += MLA_KV_RANK
    x1 = proj[:, o:o + LANES]
    x2 = proj[:, o + LANES:o + 2 * LANES]
    ss = jnp.sum(x1 * x1 + x2 * x2, axis=-1, keepdims=True) * (1.0 / (MLA_HEADS * MLA_ROPE))
    rs = lax.rsqrt(ss + EPS)
    n1 = x1 * rs * gk1_ref[...]
    n2 = x2 * rs * gk2_ref[...]
    cos = cos_ref[...]
    sin = sin_ref[...]
    lane = lax.broadcasted_iota(I32, x1.shape, 1)
    r1 = n1 * cos - n2 * sin
    r2 = n1 * sin + n2 * cos
    kpe_ref[...] = jnp.where(lane < MLA_ROPE, r1, jnp.where(lane < 2 * MLA_ROPE, r2, 0.0)).astype(BF16)


def _in_attn(x, mods, gain, w_p, gq, gk, gcq, gckv, gk1, gk2, cos_t, sin_t):
    tm = ROW_TILE
    tpb = SEQ // tm
    row = lambda cols: pl.BlockSpec((tm, cols), lambda i: (i, 0))
    full = lambda shape: pl.BlockSpec(shape, lambda i: (0,) * len(shape))
    return pl.pallas_call(
        _in_attn_body,
        grid=(TOKENS // tm,),
        in_specs=[row(D_MODEL), _mod_spec(0, tpb), _mod_spec(1, tpb), full((1, D_MODEL)),
                  full((D_MODEL, IN_ATTN_COLS)), full((1, DA_WIDTH)), full((1, DA_WIDTH)),
                  full((1, MLA_Q_RANK)), full((1, MLA_KV_RANK)), full((1, LANES)), full((1, LANES)),
                  row(LANES), row(LANES)],
        out_specs=[row(DA_WIDTH), row(DA_WIDTH), row(DA_WIDTH), row(MLA_Q_RANK), row(MLA_KV_RANK),
                   row(LANES)],
        out_shape=[jax.ShapeDtypeStruct((TOKENS, DA_WIDTH), F32)] * 3
        + [jax.ShapeDtypeStruct((TOKENS, MLA_Q_RANK), BF16),
           jax.ShapeDtypeStruct((TOKENS, MLA_KV_RANK), BF16),
           jax.ShapeDtypeStruct((TOKENS, LANES), BF16)],
        compiler_params=_params(("arbitrary",)),
        name="in_attn",
    )(x, mods, mods, gain, w_p, gq, gk, gcq, gckv, gk1, gk2, cos_t, sin_t)


MLA_PAIRS = MLA_HEADS // 2
MLA_NW = MLA_HEADS * MLA_NOPE
MLA_Q_COLS = MLA_NW + MLA_PAIRS * LANES
MLA_CAT = 2 * LANES
LOG2E = 1.4426950408889634


def _mla_up_body(cq_ref, ckv_ref, kpe_ref, wq_ref, wkv_ref, gqn_ref, gqr_ref, gkn_ref, cos_ref, sin_ref,
                 qcat_ref, kcat_ref, vext_ref):
    rows = cq_ref.shape[0]
    q = jnp.dot(cq_ref[...], wq_ref[...], preferred_element_type=F32)
    kv = jnp.dot(ckv_ref[...], wkv_ref[...], preferred_element_type=F32)
    nw = MLA_NW
    scale = (MLA_QK ** -0.5) * LOG2E
    qn = _group_norm64(q[:, 0:nw]) * gqn_ref[...] * scale
    kn = _group_norm64(kv[:, 0:nw]) * gkn_ref[...]
    v = kv[:, nw:]
    lane = lax.broadcasted_iota(I32, (rows, LANES), 1)
    lo = lane < 64
    gi = lax.broadcasted_iota(I32, (LANES, LANES), 0)
    gj = lax.broadcasted_iota(I32, (LANES, LANES), 1)
    same = ((gi % MLA_ROPE) // MLA_HALF == (gj % MLA_ROPE) // MLA_HALF) & (gi < 64) & (gj < 64)
    gmat = jnp.where(same, 1.0, 0.0).astype(BF16)
    cos = cos_ref[...]
    sin = sin_ref[...]
    kpe = kpe_ref[...]
    for p in range(MLA_PAIRS):
        x = q[:, nw + p * LANES:nw + (p + 1) * LANES]
        ssq = x * x
        hi = ssq.astype(BF16)
        low = (ssq - hi.astype(F32)).astype(BF16)
        ss = jnp.dot(hi, gmat, preferred_element_type=F32) + jnp.dot(low, gmat, preferred_element_type=F32)
        n = x * lax.rsqrt(ss * (1.0 / MLA_ROPE) + EPS) * gqr_ref[:, p * LANES:(p + 1) * LANES]
        partner = jnp.where(lane < MLA_ROPE, pltpu.roll(n, LANES - MLA_ROPE, axis=1), pltpu.roll(n, MLA_ROPE, axis=1))
        rot = jnp.where(lane < MLA_ROPE, n * cos - partner * sin, n * cos + partner * sin)
        qr = jnp.where(lo, rot * scale, 0.0)
        base = p * MLA_CAT
        qcat_ref[:, base:base + LANES] = qn[:, p * LANES:(p + 1) * LANES].astype(BF16)
        qcat_ref[:, base + LANES:base + MLA_CAT] = qr.astype(BF16)
        kcat_ref[:, base:base + LANES] = kn[:, p * LANES:(p + 1) * LANES].astype(BF16)
        kcat_ref[:, base + LANES:base + MLA_CAT] = kpe
        vp = v[:, p * LANES:(p + 1) * LANES]
        vext_ref[:, base:base + LANES] = jnp.where(lo, vp, 1.0).astype(BF16)
        vext_ref[:, base + LANES:base + MLA_CAT] = jnp.where(lo, 1.0, vp).astype(BF16)


def _mla_up(cqn, ckvn, kpe, wq_p, wkv_p, gqn, gqr, gkn, cos_t, sin_t):
    tm = ROW_TILE
    nw = MLA_NW
    cat = MLA_PAIRS * MLA_CAT
    row = lambda cols: pl.BlockSpec((tm, cols), lambda i: (i, 0))
    full = lambda shape: pl.BlockSpec(shape, lambda i: (0,) * len(shape))
    return pl.pallas_call(
        _mla_up_body,
        grid=(TOKENS // tm,),
        in_specs=[row(MLA_Q_RANK), row(MLA_KV_RANK), row(LANES), full((MLA_Q_RANK, MLA_Q_COLS)),
                  full((MLA_KV_RANK, 2 * nw)), full((1, nw)), full((1, MLA_PAIRS * LANES)),
                  full((1, nw)), row(LANES), row(LANES)],
        out_specs=[row(cat), row(cat), row(cat)],
        out_shape=[jax.ShapeDtypeStruct((TOKENS, cat), BF16)] * 3,
        compiler_params=_params(("arbitrary",)),
        name="mla_up",
    )(cqn, ckvn, kpe, wq_p, wkv_p, gqn, gqr, gkn, cos_t, sin_t)


def _mla_attn_body(q_ref, k_ref, v_ref, o_ref):
    tq, tk = MLA_TQ, MLA_TK
    qi = pl.program_id(2)
    lane = lax.broadcasted_iota(I32, (tq, MLA_CAT), 1)
    rot = lane - LANES
    head0 = (lane < 64) | ((rot >= 0) & (rot < 64) & ((rot % MLA_ROPE) < MLA_HALF))
    head1 = ((lane >= 64) & (lane < LANES)) | ((rot >= 0) & (rot < 64) & ((rot % MLA_ROPE) >= MLA_HALF))
    q = q_ref[...]
    zero = jnp.zeros_like(q)
    q2 = jnp.concatenate([jnp.where(head0, q, zero), jnp.where(head1, q, zero)], axis=0)

    def scores(ki):
        ks = pl.multiple_of(ki * tk, tk)
        return lax.dot_general(q2, k_ref[pl.ds(ks, tk), :], NT_DIMS, preferred_element_type=F32)

    def step(ki, s, carry, masked):
        ks = pl.multiple_of(ki * tk, tk)
        if masked:
            rowg = qi * tq + lax.broadcasted_iota(I32, (tq, tk), 0)
            colg = ki * tk + lax.broadcasted_iota(I32, (tq, tk), 1)
            keep = colg <= rowg
        out = []
        for h in range(2):
            m, acc = carry[h]
            sh = s[h * tq:(h + 1) * tq]
            if masked:
                sh = jnp.where(keep, sh, -jnp.inf)
            m_new = jnp.maximum(m, jnp.max(sh, axis=-1, keepdims=True))
            pb = jnp.exp2(sh - m_new).astype(BF16)
            pv = jnp.dot(pb, v_ref[pl.ds(ks, tk), h * LANES:(h + 1) * LANES], preferred_element_type=F32)
            out.append((m_new, jnp.exp2(m - m_new) * acc + pv))
        return tuple(out)

    init = tuple((jnp.full((tq, 1), -jnp.inf, F32), jnp.zeros((tq, LANES), F32)) for _ in range(2))
    kd = (qi * tq) // tk
    carry = lax.fori_loop(0, kd, lambda ki, c: step(ki, scores(ki), c, False), init)
    (_, acc0), (_, acc1) = step(kd, scores(kd), carry, True)
    half = LANES // 2
    out0 = acc0 / pltpu.roll(acc0, half, axis=1)
    out1 = acc1 / pltpu.roll(acc1, half, axis=1)
    o_ref[...] = jnp.where(lax.broadcasted_iota(I32, (tq, LANES), 1) < half, out0, out1).astype(BF16)


def _mla_attn(qcat, kcat, vext):
    tq = MLA_TQ
    nq = SEQ // tq
    return pl.pallas_call(
        _mla_attn_body,
        grid=(BATCH, MLA_PAIRS, nq),
        in_specs=[pl.BlockSpec((tq, MLA_CAT), lambda b, p, q: (b * nq + q, p)),
                  pl.BlockSpec((SEQ, MLA_CAT), lambda b, p, q: (b, p)),
                  pl.BlockSpec((SEQ, MLA_CAT), lambda b, p, q: (b, p))],
        out_specs=pl.BlockSpec((tq, LANES), lambda b, p, q: (b * nq + q, p)),
        out_shape=jax.ShapeDtypeStruct((TOKENS, MLA_HEADS * MLA_V), BF16),
        compiler_params=_params(("arbitrary", "arbitrary", "arbitrary")),
        name="mla_attn",
    )(qcat, kcat, vext)


def _dil_attn_body(q_ref, k_ref, v_ref, o_ref, os_ref, ls_ref, bias_ref):
    blk = DA_BLOCK
    lo = lax.broadcasted_iota(I32, (blk, LANES), 1) < 64
    rowi = lax.broadcasted_iota(I32, (2 * blk, 2 * blk), 0) % blk
    colj = lax.broadcasted_iota(I32, (2 * blk, 2 * blk), 1)
    rel = rowi - colj
    assert all(window // dil == blk for window, dil in DA_PATTERNS)
    for first, off in ((0, 0), (1, blk)):
        dist = off + rel
        bias_ref[first] = jnp.where((dist >= 0) & (dist <= blk), 0.0, -jnp.inf)

    for pi, (window, dil) in enumerate(DA_PATTERNS):
        n_back = window // dil
        nb = SEQ // dil // blk

        def block(t, carry, pi=pi, dil=dil, n_back=n_back, nb=nb):
            r = t // nb
            n = t % nb
            kb = jnp.maximum(n - 1, 0)
            qs = r + n * (blk * dil)
            ks = r + kb * (blk * dil)
            if dil == 1:
                qs = pl.multiple_of(qs, blk)
                ks = pl.multiple_of(ks, blk)
                q = q_ref[pl.ds(qs, blk), :]
                k2 = k_ref[pl.ds(ks, 2 * blk), :]
                v2 = v_ref[pl.ds(ks, 2 * blk), :]
            else:
                q = q_ref[pl.ds(qs, blk, stride=dil), :]
                k2 = k_ref[pl.ds(ks, 2 * blk, stride=dil), :]
                v2 = v_ref[pl.ds(ks, 2 * blk, stride=dil), :]
            zero = jnp.zeros_like(q)
            q2 = jnp.concatenate([jnp.where(lo, q, zero), jnp.where(lo, zero, q)], axis=0).astype(BF16)
            s = lax.dot_general(q2, k2.astype(BF16), NT_DIMS, preferred_element_type=F32)
            s = s + bias_ref[n - kb]
            m = jnp.max(s, axis=-1, keepdims=True)
            e = jnp.exp(s - m)
            l = jnp.sum(e, axis=-1, keepdims=True)
            pv = jnp.dot((e * (1.0 / l)).astype(BF16), v2.astype(BF16), preferred_element_type=F32)
            lse = m + jnp.log(l)
            o_blk = jnp.where(lo, pv[:blk], pv[blk:])
            l_blk = jnp.where(lo, jnp.broadcast_to(lse[:blk], (blk, LANES)),
                              jnp.broadcast_to(lse[blk:], (blk, LANES)))
            if dil == 1:
                os_ref[pi, pl.ds(qs, blk), :] = o_blk
                ls_ref[pi, pl.ds(qs, blk), :] = l_blk
            else:
                os_ref[pi, pl.ds(qs, blk, stride=dil), :] = o_blk
                ls_ref[pi, pl.ds(qs, blk, stride=dil), :] = l_blk
            return carry

        def group(g, carry, block=block):
            for u in range(DA_INTERLEAVE):
                block(g * DA_INTERLEAVE + u, carry)
            return carry

        lax.fori_loop(0, dil * nb // DA_INTERLEAVE, group, 0)

    chunk = 512

    def combine(c, carry):
        sl = pl.ds(pl.multiple_of(c * chunk, chunk), chunk)
        l0 = ls_ref[0, sl, :]
        l1 = ls_ref[1, sl, :]
        l2 = ls_ref[2, sl, :]
        mx = jnp.maximum(jnp.maximum(l0, l1), l2)
        e0 = jnp.exp(l0 - mx)
        e1 = jnp.exp(l1 - mx)
        e2 = jnp.exp(l2 - mx)
        den = e0 + e1 + e2
        o = (e0 * os_ref[0, sl, :] + e1 * os_ref[1, sl, :] + e2 * os_ref[2, sl, :]) / den
        o_ref[sl, :] = o.astype(BF16)
        return carry

    lax.fori_loop(0, SEQ // chunk, combine, 0)


def _dil_attn(qa, ka, va):
    pairs = DA_HEADS // 2
    spec = pl.BlockSpec((SEQ, LANES), lambda b, p: (b, p))
    n_pat = len(DA_PATTERNS)
    return pl.pallas_call(
        _dil_attn_body,
        grid=(BATCH, pairs),
        in_specs=[spec, spec, spec],
        out_specs=spec,
        out_shape=jax.ShapeDtypeStruct((TOKENS, DA_WIDTH), BF16),
        scratch_shapes=[pltpu.VMEM((n_pat, SEQ, LANES), F32), pltpu.VMEM((n_pat, SEQ, LANES), F32),
                        pltpu.VMEM((2, 2 * DA_BLOCK, 2 * DA_BLOCK), F32)],
        compiler_params=_params(("arbitrary", "arbitrary")),
        name="dil_attn",
    )(qa, ka, va)


def _out_proj_body(*refs, n_in, has_bias):
    a_refs = refs[:n_in]
    w_refs = refs[n_in:2 * n_in]
    rest = refs[2 * n_in:]
    if has_bias:
        b_ref, x_ref, g_ref, o_ref = rest
    else:
        x_ref, g_ref, o_ref = rest
    y = jnp.dot(a_refs[0][...], w_refs[0][...], preferred_element_type=F32)
    for a_ref, w_ref in zip(a_refs[1:], w_refs[1:]):
        y = y + jnp.dot(a_ref[...], w_ref[...], preferred_element_type=F32)
    if has_bias:
        y = y + b_ref[...]
    o_ref[...] = x_ref[...] + g_ref[0] * y


def _out_proj(acts, weights, bias, x, mods):
    tm = ROW_TILE
    tpb = SEQ // tm
    in_specs = [pl.BlockSpec((tm, a.shape[1]), lambda i: (i, 0)) for a in acts]
    in_specs += [pl.BlockSpec(w.shape, lambda i: (0, 0)) for w in weights]
    args = list(acts) + list(weights)
    if bias is not None:
        in_specs.append(pl.BlockSpec((1, D_MODEL), lambda i: (0, 0)))
        args.append(bias)
    in_specs += [pl.BlockSpec((tm, D_MODEL), lambda i: (i, 0)), _mod_spec(2, tpb)]
    args += [x, mods]
    return pl.pallas_call(
        functools.partial(_out_proj_body, n_in=len(acts), has_bias=bias is not None),
        grid=(TOKENS // tm,),
        in_specs=in_specs,
        out_specs=pl.BlockSpec((tm, D_MODEL), lambda i: (i, 0)),
        out_shape=jax.ShapeDtypeStruct((TOKENS, D_MODEL), F32),
        compiler_params=_params(("arbitrary",)),
        name="out_proj",
    )(*args)


def _in_rec_body(x_ref, sh_ref, sc_ref, g_ref, w_ref, b_ref, gate_ref, xr_ref):
    h = _prenorm(x_ref[...], g_ref[...], sh_ref[0], sc_ref[0])
    gx = jnp.dot(h.astype(BF16), w_ref[...], preferred_element_type=F32) + b_ref[...]
    gate_ref[...] = jax.nn.gelu(gx[:, :LRU_WIDTH], approximate=True)
    xr_ref[...] = gx[:, LRU_WIDTH:]


def _in_rec(x, mods, gain, w, b):
    tm = ROW_TILE
    tpb = SEQ // tm
    row = lambda cols: pl.BlockSpec((tm, cols), lambda i: (i, 0))
    full = lambda shape: pl.BlockSpec(shape, lambda i: (0,) * len(shape))
    return pl.pallas_call(
        _in_rec_body,
        grid=(TOKENS // tm,),
        in_specs=[row(D_MODEL), _mod_spec(0, tpb), _mod_spec(1, tpb), full((1, D_MODEL)),
                  full((D_MODEL, 2 * LRU_WIDTH)), full((1, 2 * LRU_WIDTH))],
        out_specs=[row(LRU_WIDTH), row(LRU_WIDTH)],
        out_shape=[jax.ShapeDtypeStruct((TOKENS, LRU_WIDTH), F32)] * 2,
        compiler_params=_params(("arbitrary",)),
        name="in_rec",
    )(x, mods, mods, gain, w, b)


def _rglru_body(xr_ref, g_ref, cw_ref, cb_ref, wa_ref, ba_ref, wx_ref, bx_ref, lam_ref, o_ref,
                tail_ref, h_ref):
    ts = SCAN_ROWS
    width = LRU_WIDTH

    @pl.when(pl.program_id(1) == 0)
    def _():
        tail_ref[...] = jnp.zeros_like(tail_ref)
        h_ref[...] = jnp.zeros_like(h_ref)

    xr = xr_ref[...]
    tail = tail_ref[...]
    row8 = lax.broadcasted_iota(I32, (SUBLANES, width), 0)
    cw = cw_ref[...]
    xc = xr * cw[CONV_WIDTH - 1:CONV_WIDTH, :] + cb_ref[...]
    for s in range(1, CONV_WIDTH):
        rolled = pltpu.roll(xr, s, axis=0)
        fix = pltpu.roll(tail, s, axis=0)
        first = jnp.where(row8 < s, fix, rolled[:SUBLANES])
        shifted = jnp.concatenate([first, rolled[SUBLANES:]], axis=0)
        xc = xc + shifted * cw[CONV_WIDTH - 1 - s:CONV_WIDTH - s, :]
    tail_ref[...] = xr[ts - SUBLANES:, :]

    xcb = xc.astype(BF16)
    bd = LRU_BLOCK_DIM
    ra = jnp.concatenate([jnp.dot(xcb[:, n * bd:(n + 1) * bd], wa_ref[n], preferred_element_type=F32)
                          for n in range(LRU_BLOCKS)], axis=1) + ba_ref[...]
    rx = jnp.concatenate([jnp.dot(xcb[:, n * bd:(n + 1) * bd], wx_ref[n], preferred_element_type=F32)
                          for n in range(LRU_BLOCKS)], axis=1) + bx_ref[...]
    r = jax.nn.sigmoid(ra)
    ig = jax.nn.sigmoid(rx)
    z = -lam_ref[...]
    softplus = jnp.maximum(z, 0.0) + jnp.log1p(jnp.exp(-jnp.abs(z)))
    log_a = (-LRU_C) * r * softplus
    a = jnp.exp(log_a)
    bt = jnp.sqrt(-jnp.tanh(log_a) * (a * a + 1.0)) * (ig * xc)

    rowi = lax.broadcasted_iota(I32, (ts, LANES), 0)
    hs = []
    for c in range(width // LANES):
        ac = a[:, c * LANES:(c + 1) * LANES]
        bc = bt[:, c * LANES:(c + 1) * LANES]
        k = 1
        while k < ts:
            keep = rowi >= k
            a_sh = jnp.where(keep, pltpu.roll(ac, k, axis=0), 1.0)
            b_sh = jnp.where(keep, pltpu.roll(bc, k, axis=0), 0.0)
            bc = ac * b_sh + bc
            ac = ac * a_sh
            k *= 2
        hs.append(ac * h_ref[0:1, c * LANES:(c + 1) * LANES] + bc)
    h = jnp.concatenate(hs, axis=1)
    h_ref[0:1, :] = h[ts - 1:ts, :]
    o_ref[...] = (g_ref[...] * h).astype(BF16)


def _rglru(xr, gate, cw, cb, wa, ba, wx, bx, lam):
    ts = SCAN_ROWS
    nt = SEQ // ts
    row = pl.BlockSpec((ts, LRU_WIDTH), lambda b, j: (b * nt + j, 0))
    full = lambda shape: pl.BlockSpec(shape, lambda b, j: (0,) * len(shape))
    vec = full((1, LRU_WIDTH))
    wspec = full((LRU_BLOCKS, LRU_BLOCK_DIM, LRU_BLOCK_DIM))
    return pl.pallas_call(
        _rglru_body,
        grid=(BATCH, nt),
        in_specs=[row, row, full((CONV_WIDTH, LRU_WIDTH)), vec, wspec, vec, wspec, vec, vec],
        out_specs=row,
        out_shape=jax.ShapeDtypeStruct((TOKENS, LRU_WIDTH), BF16),
        scratch_shapes=[pltpu.VMEM((SUBLANES, LRU_WIDTH), F32), pltpu.VMEM((SUBLANES, LRU_WIDTH), F32)],
        compiler_params=_params(("arbitrary", "arbitrary")),
        name="rglru",
    )(xr, gate, cw, cb, wa, ba, wx, bx, lam)


def _router_body(x_ref, sh_ref, sc_ref, g_ref, wr_ref, br_ref,
                 hn_ref, idx_ref, rank_ref, gate_ref, cnt_ref, carry_ref):
    tm = ROW_TILE

    @pl.when(pl.program_id(0) == 0)
    def _():
        carry_ref[...] = jnp.zeros_like(carry_ref)

    h = _prenorm(x_ref[...], g_ref[...], sh_ref[0], sc_ref[0])
    _store_row_tiles(hn_ref, h)
    hb = h.astype(BF16)
    hl = (h - hb.astype(F32)).astype(BF16)
    wr = wr_ref[...]
    wh = wr.astype(BF16)
    wl = (wr - wh.astype(F32)).astype(BF16)
    logits = (jnp.dot(hb, wh, preferred_element_type=F32) + jnp.dot(hb, wl, preferred_element_type=F32)
              + jnp.dot(hl, wh, preferred_element_type=F32)) + br_ref[...]
    lane = lax.broadcasted_iota(I32, (tm, LANES), 1)
    lanef = lane.astype(F32)
    neg = -jnp.inf
    work = logits
    tops, hots = [], []
    for _ in range(TOP_K):
        m = jnp.max(work, axis=-1, keepdims=True)
        first = jnp.min(jnp.where(work == m, lanef, float(LANES)), axis=-1, keepdims=True)
        hot = lanef == first
        work = jnp.where(hot, neg, work)
        tops.append((m, first))
        hots.append(hot)
    exps = [jnp.exp(m - tops[0][0]) for m, _ in tops]
    den = exps[0] + exps[1] + exps[2] + exps[3]
    sel = jnp.zeros((tm, LANES), F32)
    for hot in hots:
        sel = jnp.where(hot, 1.0, sel)
    ri = lax.broadcasted_iota(I32, (tm, tm), 0)
    ci = lax.broadcasted_iota(I32, (tm, tm), 1)
    tril = jnp.where(ci < ri, 1.0, 0.0).astype(BF16)
    cum = jnp.dot(tril, sel.astype(BF16), preferred_element_type=F32) + carry_ref[0:1, :]
    idx_slab = jnp.zeros((tm, LANES), I32)
    rank_slab = jnp.zeros((tm, LANES), I32)
    gate_slab = jnp.zeros((tm, LANES), F32)
    for k in range(TOP_K):
        rank = jnp.sum(jnp.where(hots[k], cum, 0.0), axis=-1, keepdims=True)
        idx_slab = jnp.where(lane == k, tops[k][1].astype(I32), idx_slab)
        rank_slab = jnp.where(lane == k, rank.astype(I32), rank_slab)
        gate_slab = jnp.where(lane == k, exps[k] / den, gate_slab)
    idx_ref[...] = idx_slab
    rank_ref[...] = rank_slab
    gate_ref[...] = gate_slab
    total = carry_ref[0:1, :] + jnp.sum(sel, axis=0, keepdims=True)
    carry_ref[0:1, :] = total
    cnt_ref[...] = total


def _router(x, mods, gain, wr_p, br_p):
    tm = ROW_TILE
    tpb = SEQ // tm
    row = lambda cols: pl.BlockSpec((tm, cols), lambda i: (i, 0))
    full = lambda shape: pl.BlockSpec(shape, lambda i: (0,) * len(shape))
    return pl.pallas_call(
        _router_body,
        grid=(TOKENS // tm,),
        in_specs=[row(D_MODEL), _mod_spec(3, tpb), _mod_spec(4, tpb), full((1, D_MODEL)),
                  full((D_MODEL, LANES)), full((1, LANES))],
        out_specs=[pl.BlockSpec((tm * ROW_CHUNKS, LANES), lambda i: (i, 0)),
                   row(LANES), row(LANES), row(LANES), full((1, LANES))],
        out_shape=[jax.ShapeDtypeStruct((TOKENS * ROW_CHUNKS, LANES), F32),
                   jax.ShapeDtypeStruct((TOKENS, LANES), I32),
                   jax.ShapeDtypeStruct((TOKENS, LANES), I32),
                   jax.ShapeDtypeStruct((TOKENS, LANES), F32),
                   jax.ShapeDtypeStruct((1, LANES), F32)],
        scratch_shapes=[pltpu.VMEM((SUBLANES, LANES), F32)],
        compiler_params=_params(("arbitrary",)),
        name="router",
    )(x, mods, mods, gain, wr_p, br_p)


def _tile_copy(src, src_row, dst, dst_row, sem):
    s0 = pl.multiple_of(src_row * ROW_CHUNKS, ROW_CHUNKS)
    d0 = pl.multiple_of(dst_row * ROW_CHUNKS, ROW_CHUNKS)
    return pltpu.make_async_copy(src.at[pl.ds(s0, ROW_CHUNKS), :], dst.at[pl.ds(d0, ROW_CHUNKS), :], sem)


def _store_row_tiles(ref, val):
    rows = val.shape[0]
    for c in range(ROW_CHUNKS):
        ref[pl.ds(c, rows, stride=ROW_CHUNKS), :] = val[:, c * LANES:(c + 1) * LANES]


def _load_row_tiles(ref, rows):
    return jnp.concatenate([ref[pl.ds(c, rows, stride=ROW_CHUNKS), :] for c in range(ROW_CHUNKS)], axis=1)


def _moe_dispatch_body(padbase_ref, npad_ref, nused_ref, dest_ref, hn_ref, xs_hbm, zero_ref, sem, *, n_blocks):
    td = ROW_TILE
    block_rows = MOE_ROWS * ROW_CHUNKS

    @pl.when(pl.program_id(0) == 0)
    def _():
        zero_ref[...] = jnp.zeros_like(zero_ref)

        def per_expert(e, carry, wait):
            n = npad_ref[e]
            size = MOE_ROWS // 2
            while size >= 1:
                @pl.when((n & size) != 0)
                def _(size=size):
                    first = padbase_ref[e] + (n & ~(2 * size - 1))
                    start = pl.multiple_of(first * ROW_CHUNKS, ROW_CHUNKS)
                    cp = pltpu.make_async_copy(zero_ref.at[pl.ds(0, size * ROW_CHUNKS), :],
                                               xs_hbm.at[pl.ds(start, size * ROW_CHUNKS), :], sem)
                    if wait:
                        cp.wait()
                    else:
                        cp.start()

                size //= 2
            return carry

        def per_block(b, carry, wait):
            start = pl.multiple_of(b * block_rows, block_rows)
            cp = pltpu.make_async_copy(zero_ref, xs_hbm.at[pl.ds(start, block_rows), :], sem)
            if wait:
                cp.wait()
            else:
                cp.start()
            return carry

        for wait in (False, True):
            lax.fori_loop(0, N_EXPERTS, functools.partial(per_expert, wait=wait), 0)
            lax.fori_loop(nused_ref[0], n_blocks, functools.partial(per_block, wait=wait), 0)

    def issue(r, carry):
        for k in range(TOP_K):
            _tile_copy(hn_ref, r, xs_hbm, dest_ref[0, 0, r * TOP_K + k], sem).start(priority=k % 2)
        return carry

    lax.fori_loop(0, td, issue, 0, unroll=8)

    def drain(r, carry):
        for k in range(TOP_K):
            _tile_copy(hn_ref, r, xs_hbm, 0, sem).wait()
        return carry

    lax.fori_loop(0, td, drain, 0, unroll=8)


def _moe_dispatch(pad_base, n_pad, n_used, dest, hn_tiles, n_blocks):
    td = ROW_TILE
    nt = TOKENS // td
    grid_spec = pltpu.PrefetchScalarGridSpec(
        num_scalar_prefetch=3,
        grid=(nt,),
        in_specs=[pl.BlockSpec((1, 1, td * TOP_K), lambda i, *_: (i, 0, 0), memory_space=pltpu.SMEM),
                  pl.BlockSpec((td * ROW_CHUNKS, LANES), lambda i, *_: (i, 0))],
        out_specs=pl.BlockSpec(memory_space=pl.ANY),
        scratch_shapes=[pltpu.VMEM((MOE_ROWS * ROW_CHUNKS, LANES), F32), pltpu.SemaphoreType.DMA],
    )
    return pl.pallas_call(
        functools.partial(_moe_dispatch_body, n_blocks=n_blocks),
        grid_spec=grid_spec,
        out_shape=jax.ShapeDtypeStruct((n_blocks * MOE_ROWS * ROW_CHUNKS, LANES), F32),
        compiler_params=_params(("arbitrary",)),
        name="moe_dispatch",
    )(pad_base, n_pad, n_used, dest.reshape(nt, 1, td * TOP_K), hn_tiles)


def _moe_ffn_body(be_ref, nused_ref, xs_ref, w1_ref, b1_ref, w2_ref, b2_ref, ys_ref, ht_ref):
    bm = MOE_ROWS
    i = pl.program_id(0)

    @pl.when(i < nused_ref[0])
    def _():
        h = jnp.dot(_load_row_tiles(xs_ref, bm), w1_ref[0], preferred_element_type=F32) + b1_ref[0]
        ht = h.T
        for j in range(bm // LANES):
            ht_ref[j] = ht[:, j * LANES:(j + 1) * LANES]
        parts = []
        for j in range(bm // LANES):
            gate = jnp.minimum(ht_ref[j, pl.ds(0, D_EXPERT, stride=2), :], SWIGLU_LIMIT)
            lin = jnp.clip(ht_ref[j, pl.ds(1, D_EXPERT, stride=2), :], -SWIGLU_LIMIT, SWIGLU_LIMIT)
            parts.append(gate * jax.nn.sigmoid(SWIGLU_ALPHA * gate) * (lin + 1.0))
        act_t = jnp.concatenate(parts, axis=1)
        y = lax.dot_general(act_t, w2_ref[0], (((0,), (0,)), ((), ())), preferred_element_type=F32)
        _store_row_tiles(ys_ref, y + b2_ref[0])

    @pl.when(i >= nused_ref[0])
    def _():
        ys_ref[...] = jnp.zeros_like(ys_ref)


def _moe_ffn(block_expert, n_used, xs, w1, b1, w2, b2, layer, n_blocks):
    bm = MOE_ROWS
    f2 = 2 * D_EXPERT
    base = layer * N_EXPERTS
    grid_spec = pltpu.PrefetchScalarGridSpec(
        num_scalar_prefetch=2,
        grid=(n_blocks,),
        in_specs=[pl.BlockSpec((bm * ROW_CHUNKS, LANES), lambda i, be, nu: (jnp.minimum(i, nu[0] - 1), 0)),
                  pl.BlockSpec((1, D_MODEL, f2), lambda i, be, nu: (base + be[i], 0, 0)),
                  pl.BlockSpec((1, 1, f2), lambda i, be, nu: (base + be[i], 0, 0)),
                  pl.BlockSpec((1, D_EXPERT, D_MODEL), lambda i, be, nu: (base + be[i], 0, 0)),
                  pl.BlockSpec((1, 1, D_MODEL), lambda i, be, nu: (base + be[i], 0, 0))],
        out_specs=pl.BlockSpec((bm * ROW_CHUNKS, LANES), lambda i, be, nu: (i, 0)),
        scratch_shapes=[pltpu.VMEM((bm // LANES, f2, LANES), F32)],
    )
    return pl.pallas_call(
        _moe_ffn_body,
        grid_spec=grid_spec,
        out_shape=jax.ShapeDtypeStruct((n_blocks * bm * ROW_CHUNKS, LANES), F32),
        compiler_params=_params(("arbitrary",)),
        name="moe_ffn",
    )(block_expert, n_used, xs,
      w1.reshape(DEPTH * N_EXPERTS, D_MODEL, f2), b1.reshape(DEPTH * N_EXPERTS, 1, f2),
      w2.reshape(DEPTH * N_EXPERTS, D_EXPERT, D_MODEL), b2.reshape(DEPTH * N_EXPERTS, 1, D_MODEL))


def _moe_combine_body(dest_ref, dnext_ref, gates_ref, x_ref, g_ref, ys_hbm, o_ref, buf0_ref, buf1_ref, sem):
    tc = COMBINE_ROWS
    i = pl.program_id(0)
    bufs = (buf0_ref, buf1_ref)

    def issue(d_ref, slot):
        def body(r, carry):
            for k in range(TOP_K):
                _tile_copy(ys_hbm, d_ref[0, 0, r * TOP_K + k], bufs[slot].at[k], r,
                           sem.at[slot]).start(priority=k % 2)
            return carry

        lax.fori_loop(0, tc, body, 0, unroll=8)

    def drain(slot):
        def body(r, carry):
            for k in range(TOP_K):
                _tile_copy(ys_hbm, 0, bufs[slot].at[k], r, sem.at[slot]).wait()
            return carry

        lax.fori_loop(0, tc, body, 0, unroll=8)

    def finish(slot):
        drain(slot)
        gates = gates_ref[...]
        gate = g_ref[0]
        for c in range(ROW_CHUNKS):
            rows = pl.ds(c, tc, stride=ROW_CHUNKS)
            y = gates[:, 0:1] * bufs[slot][0, rows, :]
            for k in range(1, TOP_K):
                y = y + gates[:, k:k + 1] * bufs[slot][k, rows, :]
            cols = slice(c * LANES, (c + 1) * LANES)
            o_ref[:, cols] = x_ref[:, cols] + gate[:, cols] * y

    @pl.when(i == 0)
    def _():
        issue(dest_ref, 0)

    for slot in range(2):
        @pl.when(i % 2 == slot)
        def _(slot=slot):
            @pl.when(i + 1 < pl.num_programs(0))
            def _():
                issue(dnext_ref, 1 - slot)

            finish(slot)


def _moe_combine(dest, gates, x, mods, ys):
    tc = COMBINE_ROWS
    tpb = SEQ // tc
    nt = TOKENS // tc
    dest3 = dest.reshape(nt, 1, tc * TOP_K)
    buf = pltpu.VMEM((TOP_K, tc * ROW_CHUNKS, LANES), F32)
    return pl.pallas_call(
        _moe_combine_body,
        grid=(nt,),
        in_specs=[pl.BlockSpec((1, 1, tc * TOP_K), lambda i: (i, 0, 0), memory_space=pltpu.SMEM),
                  pl.BlockSpec((1, 1, tc * TOP_K), lambda i: (jnp.minimum(i + 1, nt - 1), 0, 0),
                               memory_space=pltpu.SMEM),
                  pl.BlockSpec((tc, LANES), lambda i: (i, 0)),
                  pl.BlockSpec((tc, D_MODEL), lambda i: (i, 0)),
                  _mod_spec(5, tpb),
                  pl.BlockSpec(memory_space=pl.ANY)],
        out_specs=pl.BlockSpec((tc, D_MODEL), lambda i: (i, 0)),
        out_shape=jax.ShapeDtypeStruct((TOKENS, D_MODEL), F32),
        scratch_shapes=[buf, buf, pltpu.SemaphoreType.DMA((2,))],
        compiler_params=_params(("arbitrary",)),
        name="moe_combine",
    )(dest3, dest3, gates, x, mods, ys)


def _attn_layer(x, mods, gain, j, cos_t, sin_t, p):
    w_in = p["attn_w_in"][j]
    rope_cols = A_COLS + MLA_Q_RANK + MLA_KV_RANK
    w_p = jnp.concatenate(
        [w_in[:, :rope_cols],
         jnp.tile(w_in[:, rope_cols:rope_cols + MLA_HALF], (1, MLA_HEADS)),
         jnp.tile(w_in[:, rope_cols + MLA_HALF:], (1, MLA_HEADS))], axis=1).astype(BF16)
    tile_h = lambda v: jnp.tile(v, MLA_HEADS).reshape(1, -1)
    kg = p["mla_k_gain"][j]
    qg = p["mla_q_gain"][j]
    qa, ka, va, cqn, ckvn, kpe = _in_attn(
        x, mods, gain, w_p, tile_h(p["da_q_gain"][j]), tile_h(p["da_k_gain"][j]),
        p["mla_qa_gain"][j].reshape(1, -1), p["mla_kva_gain"][j].reshape(1, -1),
        tile_h(kg[MLA_NOPE:MLA_NOPE + MLA_HALF]), tile_h(kg[MLA_NOPE + MLA_HALF:]), cos_t, sin_t)

    wq = p["mla_w_uq"][j].reshape(MLA_Q_RANK, MLA_HEADS, MLA_QK)
    first = slice(MLA_NOPE, MLA_NOPE + MLA_HALF)
    second = slice(MLA_NOPE + MLA_HALF, MLA_QK)
    pad_cols = jnp.zeros((MLA_Q_RANK, LANES - 2 * MLA_ROPE), F32)
    rot_blocks = [jnp.concatenate([wq[:, 2 * q, first], wq[:, 2 * q + 1, first],
                                   wq[:, 2 * q, second], wq[:, 2 * q + 1, second], pad_cols], axis=1)
                  for q in range(MLA_PAIRS)]
    wq_p = jnp.concatenate([wq[:, :, :MLA_NOPE].reshape(MLA_Q_RANK, -1)] + rot_blocks, axis=1).astype(BF16)
    gqr = jnp.tile(jnp.concatenate([qg[first], qg[first], qg[second], qg[second],
                                    jnp.zeros((LANES - 2 * MLA_ROPE,), F32)]), MLA_PAIRS).reshape(1, -1)
    wkv = p["mla_w_ukv"][j].reshape(MLA_KV_RANK, MLA_HEADS, MLA_NOPE + MLA_V)
    wkv_p = jnp.concatenate(
        [wkv[:, :, :MLA_NOPE].reshape(MLA_KV_RANK, -1),
         wkv[:, :, MLA_NOPE:].reshape(MLA_KV_RANK, -1)], axis=1).astype(BF16)
    qcat, kcat, vext = _mla_up(cqn, ckvn, kpe, wq_p, wkv_p, tile_h(qg[:MLA_NOPE]), gqr,
                               tile_h(kg[:MLA_NOPE]), cos_t, sin_t)

    o_a = _dil_attn(qa, ka, va)
    o_b = _mla_attn(qcat, kcat, vext)
    w_out = p["attn_w_out"][j].astype(BF16)
    return _out_proj([o_a, o_b], [w_out[:DA_WIDTH], w_out[DA_WIDTH:]], None, x, mods)


def _rec_layer(x, mods, gain, j, p):
    gate, xr = _in_rec(x, mods, gain, p["rec_w_in"][j].astype(BF16), p["rec_b_in"][j].reshape(1, -1))
    gy = _rglru(xr, gate, p["rec_conv_w"][j], p["rec_conv_b"][j].reshape(1, -1),
                p["rec_wa"][j].astype(BF16), p["rec_ba"][j].reshape(1, -1),
                p["rec_wx"][j].astype(BF16), p["rec_bx"][j].reshape(1, -1),
                p["rec_lambda"][j].reshape(1, -1))
    return _out_proj([gy], [p["rec_w_out"][j].astype(BF16)], p["rec_b_out"][j].reshape(1, -1), x, mods)


def _moe_layer(x, mods, gain, layer, p):
    bm = MOE_ROWS
    wr_p = jnp.pad(p["moe_w_router"][layer], ((0, 0), (0, LANES - N_EXPERTS)))
    br_p = jnp.pad(p["moe_b_router"][layer], (0, LANES - N_EXPERTS), constant_values=-jnp.inf).reshape(1, LANES)
    hn, idx_slab, rank_slab, gate_slab, cnt = _router(x, mods, gain, wr_p, br_p)

    counts = cnt[0, :N_EXPERTS].astype(I32)
    padded = (counts + bm - 1) // bm * bm
    pad_ends = jnp.cumsum(padded)
    pad_starts = pad_ends - padded
    n_blocks = TOKENS * TOP_K // bm + N_EXPERTS
    experts = jnp.arange(N_EXPERTS, dtype=I32)
    idx = idx_slab[:, :TOP_K]
    dest = rank_slab[:, :TOP_K] + jnp.sum(
        jnp.where(idx[:, :, None] == experts, pad_starts, 0), axis=-1, dtype=I32)
    block_start = jnp.arange(n_blocks, dtype=I32) * bm
    block_expert = jnp.minimum(
        jnp.sum(pad_ends[None, :] <= block_start[:, None], axis=-1, dtype=I32), N_EXPERTS - 1)
    n_used = (pad_ends[-1:] // bm).astype(I32)

    xs = _moe_dispatch(pad_starts + counts, padded - counts, n_used, dest, hn, n_blocks)
    ys = _moe_ffn(block_expert, n_used, xs, p["moe_w1"], p["moe_b1"], p["moe_w2"], p["moe_b2"], layer, n_blocks)
    return _moe_combine(dest, gate_slab, x, mods, ys)


def kernel(x, c, positions, ada_w, ada_b, norm1_gain, norm2_gain, attn_w_in, da_q_gain, da_k_gain, mla_qa_gain, mla_kva_gain, mla_w_uq, mla_w_ukv, mla_q_gain, mla_k_gain, attn_w_out, rec_w_in, rec_b_in, rec_conv_w, rec_conv_b, rec_wa, rec_ba, rec_wx, rec_bx, rec_lambda, rec_w_out, rec_b_out, moe_w_router, moe_b_router, moe_w1, moe_b1, moe_w2, moe_b2):
    p = dict(attn_w_in=attn_w_in, da_q_gain=da_q_gain, da_k_gain=da_k_gain, mla_qa_gain=mla_qa_gain,
             mla_kva_gain=mla_kva_gain, mla_w_uq=mla_w_uq, mla_w_ukv=mla_w_ukv, mla_q_gain=mla_q_gain,
             mla_k_gain=mla_k_gain, attn_w_out=attn_w_out, rec_w_in=rec_w_in, rec_b_in=rec_b_in,
             rec_conv_w=rec_conv_w, rec_conv_b=rec_conv_b, rec_wa=rec_wa, rec_ba=rec_ba, rec_wx=rec_wx,
             rec_bx=rec_bx, rec_lambda=rec_lambda, rec_w_out=rec_w_out, rec_b_out=rec_b_out,
             moe_w_router=moe_w_router, moe_b_router=moe_b_router, moe_w1=moe_w1, moe_b1=moe_b1,
             moe_w2=moe_w2, moe_b2=moe_b2)
    mods_all = _ada(c, ada_w, ada_b)
    cos_t, sin_t = _rope_tables(positions)
    xt = x.reshape(TOKENS, D_MODEL)
    for layer in range(DEPTH):
        mods = mods_all[layer]
        j = layer // 2
        gain1 = norm1_gain[layer].reshape(1, D_MODEL)
        if layer % 2 == 0:
            xt = _attn_layer(xt, mods, gain1, j, cos_t, sin_t, p)
        else:
            xt = _rec_layer(xt, mods, gain1, j, p)
        xt = _moe_layer(xt, mods, norm2_gain[layer].reshape(1, D_MODEL), layer, p)
    return xt.reshape(BATCH, SEQ, D_MODEL)
```

```python
import functools

import jax
import jax.numpy as jnp
from jax import lax
from jax.experimental import pallas as pl
from jax.experimental.pallas import tpu as pltpu

F32 = jnp.float32
BF16 = jnp.bfloat16
I32 = jnp.int32

D_MODEL = 1024
BATCH = 4
SEQ = 4096
TOKENS = BATCH * SEQ
DEPTH = 4
EPS = 1e-6

DA_HEADS = 8
DA_HEAD_DIM = 64
DA_PATTERNS = ((128, 1), (512, 4), (2048, 16))
DA_BLOCK = 128
DA_WIDTH = DA_HEADS * DA_HEAD_DIM
DA_INTERLEAVE = 32

MLA_HEADS = 8
MLA_Q_RANK = 256
MLA_KV_RANK = 128
MLA_NOPE = 64
MLA_ROPE = 32
MLA_HALF = MLA_ROPE // 2
MLA_V = 64
MLA_QK = MLA_NOPE + MLA_ROPE
ROPE_THETA = 10000.0
A_COLS = 3 * DA_WIDTH

LRU_WIDTH = D_MODEL
LRU_BLOCKS = 8
LRU_BLOCK_DIM = LRU_WIDTH // LRU_BLOCKS
CONV_WIDTH = 4
LRU_C = 8.0

N_EXPERTS = 32
TOP_K = 4
D_EXPERT = D_MODEL
SWIGLU_LIMIT = 7.0
SWIGLU_ALPHA = 1.702

LANES = 128
SUBLANES = 8
ROW_CHUNKS = D_MODEL // LANES
assert ROW_CHUNKS == SUBLANES
VMEM_LIMIT = 56 * 1024 * 1024

ROW_TILE = 512
MOE_ROWS = 512
DISPATCH_ROWS = 512
COMBINE_ROWS = 256
SCAN_ROWS = 128
MLA_TQ = 1024
MLA_TK = 1024
assert MLA_TK % MLA_TQ == 0

NT_DIMS = (((1,), (1,)), ((), ()))


def _params(sem, vmem=VMEM_LIMIT):
    return pltpu.CompilerParams(dimension_semantics=sem, vmem_limit_bytes=vmem)


def _prenorm(x, gain, shift, scale):
    ms = jnp.mean(x * x, axis=-1, keepdims=True)
    return (x * lax.rsqrt(ms + EPS) * gain) * (1.0 + scale) + shift


def _group_norm64(y):
    rows = y.shape[0]
    lo = lax.broadcasted_iota(I32, (rows, LANES), 1) < 64
    outs = []
    for j in range(y.shape[1] // LANES):
        c = y[:, j * LANES:(j + 1) * LANES]
        sq = c * c
        s0 = jnp.sum(jnp.where(lo, sq, 0.0), axis=-1, keepdims=True)
        s1 = jnp.sum(jnp.where(lo, 0.0, sq), axis=-1, keepdims=True)
        r = jnp.where(lo, lax.rsqrt(s0 * (1.0 / 64) + EPS), lax.rsqrt(s1 * (1.0 / 64) + EPS))
        outs.append(c * r)
    return jnp.concatenate(outs, axis=-1)


def _ada_body(c_ref, w_ref, b_ref, o_ref):
    c = c_ref[...]
    ca = c * jax.nn.sigmoid(c)
    o_ref[0] = jnp.dot(ca.astype(BF16), w_ref[0].astype(BF16), preferred_element_type=F32) + b_ref[0]


def _ada(c, ada_w, ada_b):
    c8 = jnp.pad(c, ((0, SUBLANES - BATCH), (0, 0)))
    out = pl.pallas_call(
        _ada_body,
        grid=(DEPTH, 6),
        in_specs=[pl.BlockSpec((SUBLANES, D_MODEL), lambda l, j: (0, 0)),
                  pl.BlockSpec((1, D_MODEL, D_MODEL), lambda l, j: (l, 0, j)),
                  pl.BlockSpec((1, 1, D_MODEL), lambda l, j: (l, 0, j))],
        out_specs=pl.BlockSpec((1, SUBLANES, D_MODEL), lambda l, j: (l, 0, j)),
        out_shape=jax.ShapeDtypeStruct((DEPTH, SUBLANES, 6 * D_MODEL), F32),
        compiler_params=_params(("arbitrary", "arbitrary")),
        name="ada",
    )(c8, ada_w, ada_b.reshape(DEPTH, 1, 6 * D_MODEL))
    return out[:, :BATCH].reshape(DEPTH, BATCH * 6, 1, D_MODEL)


def _mod_spec(j, rows_per_batch_tiles):
    return pl.BlockSpec((1, 1, D_MODEL), lambda i: ((i // rows_per_batch_tiles) * 6 + j, 0, 0))


def _rope_body(pos_ref, inv_ref, cos_ref, sin_ref):
    ang = pos_ref[...].astype(F32) * inv_ref[...]
    cos_ref[...] = jnp.cos(ang)
    sin_ref[...] = jnp.sin(ang)


def _rope_tables(positions):
    inv = ROPE_THETA ** (-jnp.arange(0, MLA_ROPE, 2, dtype=F32) / MLA_ROPE)
    inv_t = jnp.tile(inv, MLA_HEADS).reshape(1, LANES)
    tm = 512
    return pl.pallas_call(
        _rope_body,
        grid=(TOKENS // tm,),
        in_specs=[pl.BlockSpec((tm, 1), lambda i: (i, 0)),
                  pl.BlockSpec((1, LANES), lambda i: (0, 0))],
        out_specs=[pl.BlockSpec((tm, LANES), lambda i: (i, 0))] * 2,
        out_shape=[jax.ShapeDtypeStruct((TOKENS, LANES), F32)] * 2,
        compiler_params=_params(("arbitrary",)),
        name="rope_tables",
    )(positions.reshape(TOKENS, 1), inv_t)


IN_ATTN_COLS = A_COLS + MLA_Q_RANK + MLA_KV_RANK + 2 * LANES


def _in_attn_body(x_ref, sh_ref, sc_ref, g_ref, w_ref, gq_ref, gk_ref, gcq_ref, gckv_ref,
                  gk1_ref, gk2_ref, cos_ref, sin_ref,
                  qa_ref, ka_ref, va_ref, cq_ref, ckv_ref, kpe_ref):
    h = _prenorm(x_ref[...], g_ref[...], sh_ref[0], sc_ref[0])
    proj = jnp.dot(h.astype(BF16), w_ref[...], preferred_element_type=F32)
    w = DA_WIDTH
    qa_ref[...] = _group_norm64(proj[:, 0:w]) * gq_ref[...] * (DA_HEAD_DIM ** -0.5)
    ka_ref[...] = _group_norm64(proj[:, w:2 * w]) * gk_ref[...]
    va_ref[...] = proj[:, 2 * w:3 * w]
    o = A_COLS
    cq = proj[:, o:o + MLA_Q_RANK]
    cq_ref[...] = (cq * lax.rsqrt(jnp.mean(cq * cq, axis=-1, keepdims=True) + EPS) * gcq_ref[...]).astype(BF16)
    o += MLA_Q_RANK
    ckv = proj[:, o:o + MLA_KV_RANK]
    ckv_ref[...] = (ckv * lax.rsqrt(jnp.mean(ckv * ckv, axis=-1, keepdims=True) + EPS) * gckv_ref[...]).astype(BF16)
    o += MLA_KV_RANK
    x1 = proj[:, o:o + LANES]
    x2 = proj[:, o + LANES:o + 2 * LANES]
    ss = jnp.sum(x1 * x1 + x2 * x2, axis=-1, keepdims=True) * (1.0 / (MLA_HEADS * MLA_ROPE))
    rs = lax.rsqrt(ss + EPS)
    n1 = x1 * rs * gk1_ref[...]
    n2 = x2 * rs * gk2_ref[...]
    cos = cos_ref[...]
    sin = sin_ref[...]
    lane = lax.broadcasted_iota(I32, x1.shape, 1)
    r1 = n1 * cos - n2 * sin
    r2 = n1 * sin + n2 * cos
    kpe_ref[...] = jnp.where(lane < MLA_ROPE, r1, jnp.where(lane < 2 * MLA_ROPE, r2, 0.0)).astype(BF16)


def _in_attn(x, mods, gain, w_p, gq, gk, gcq, gckv, gk1, gk2, cos_t, sin_t):
    tm = ROW_TILE
    tpb = SEQ // tm
    row = lambda cols: pl.BlockSpec((tm, cols), lambda i: (i, 0))
    full = lambda shape: pl.BlockSpec(shape, lambda i: (0,) * len(shape))
    return pl.pallas_call(
        _in_attn_body,
        grid=(TOKENS // tm,),
        in_specs=[row(D_MODEL), _mod_spec(0, tpb), _mod_spec(1, tpb), full((1, D_MODEL)),
                  full((D_MODEL, IN_ATTN_COLS)), full((1, DA_WIDTH)), full((1, DA_WIDTH)),
                  full((1, MLA_Q_RANK)), full((1, MLA_KV_RANK)), full((1, LANES)), full((1, LANES)),
                  row(LANES), row(LANES)],
        out_specs=[row(DA_WIDTH), row(DA_WIDTH), row(DA_WIDTH), row(MLA_Q_RANK), row(MLA_KV_RANK),
                   row(LANES)],
        out_shape=[jax.ShapeDtypeStruct((TOKENS, DA_WIDTH), F32)] * 3
        + [jax.ShapeDtypeStruct((TOKENS, MLA_Q_RANK), BF16),
           jax.ShapeDtypeStruct((TOKENS, MLA_KV_RANK), BF16),
           jax.ShapeDtypeStruct((TOKENS, LANES), BF16)],
        compiler_params=_params(("arbitrary",)),
        name="in_attn",
    )(x, mods, mods, gain, w_p, gq, gk, gcq, gckv, gk1, gk2, cos_t, sin_t)


MLA_PAIRS = MLA_HEADS // 2
MLA_NW = MLA_HEADS * MLA_NOPE
MLA_Q_COLS = MLA_NW + MLA_PAIRS * LANES
MLA_CAT = 2 * LANES
LOG2E = 1.4426950408889634


def _mla_up_body(cq_ref, ckv_ref, kpe_ref, wq_ref, wkv_ref, gqn_ref, gqr_ref, gkn_ref, cos_ref, sin_ref,
                 qcat_ref, kcat_ref, vext_ref):
    rows = cq_ref.shape[0]
    q = jnp.dot(cq_ref[...], wq_ref[...], preferred_element_type=F32)
    kv = jnp.dot(ckv_ref[...], wkv_ref[...], preferred_element_type=F32)
    nw = MLA_NW
    scale = (MLA_QK ** -0.5) * LOG2E
    qn = _group_norm64(q[:, 0:nw]) * gqn_ref[...] * scale
    kn = _group_norm64(kv[:, 0:nw]) * gkn_ref[...]
    v = kv[:, nw:]
    lane = lax.broadcasted_iota(I32, (rows, LANES), 1)
    lo = lane < 64
    gi = lax.broadcasted_iota(I32, (LANES, LANES), 0)
    gj = lax.broadcasted_iota(I32, (LANES, LANES), 1)
    same = ((gi % MLA_ROPE) // MLA_HALF == (gj % MLA_ROPE) // MLA_HALF) & (gi < 64) & (gj < 64)
    gmat = jnp.where(same, 1.0, 0.0).astype(BF16)
    cos = cos_ref[...]
    sin = sin_ref[...]
    kpe = kpe_ref[...]
    for p in range(MLA_PAIRS):
        x = q[:, nw + p * LANES:nw + (p + 1) * LANES]
        ssq = x * x
        hi = ssq.astype(BF16)
        low = (ssq - hi.astype(F32)).astype(BF16)
        ss = jnp.dot(hi, gmat, preferred_element_type=F32) + jnp.dot(low, gmat, preferred_element_type=F32)
        n = x * lax.rsqrt(ss * (1.0 / MLA_ROPE) + EPS) * gqr_ref[:, p * LANES:(p + 1) * LANES]
        partner = jnp.where(lane < MLA_ROPE, pltpu.roll(n, LANES - MLA_ROPE, axis=1), pltpu.roll(n, MLA_ROPE, axis=1))
        rot = jnp.where(lane < MLA_ROPE, n * cos - partner * sin, n * cos + partner * sin)
        qr = jnp.where(lo, rot * scale, 0.0)
        base = p * MLA_CAT
        qcat_ref[:, base:base + LANES] = qn[:, p * LANES:(p + 1) * LANES].astype(BF16)
        qcat_ref[:, base + LANES:base + MLA_CAT] = qr.astype(BF16)
        kcat_ref[:, base:base + LANES] = kn[:, p * LANES:(p + 1) * LANES].astype(BF16)
        kcat_ref[:, base + LANES:base + MLA_CAT] = kpe
        vp = v[:, p * LANES:(p + 1) * LANES]
        vext_ref[:, base:base + LANES] = jnp.where(lo, vp, 1.0).astype(BF16)
        vext_ref[:, base + LANES:base + MLA_CAT] = jnp.where(lo, 1.0, vp).astype(BF16)


def _mla_up(cqn, ckvn, kpe, wq_p, wkv_p, gqn, gqr, gkn, cos_t, sin_t):
    tm = ROW_TILE
    nw = MLA_NW
    cat = MLA_PAIRS * MLA_CAT
    row = lambda cols: pl.BlockSpec((tm, cols), lambda i: (i, 0))
    full = lambda shape: pl.BlockSpec(shape, lambda i: (0,) * len(shape))
    return pl.pallas_call(
        _mla_up_body,
        grid=(TOKENS // tm,),
        in_specs=[row(MLA_Q_RANK), row(MLA_KV_RANK), row(LANES), full((MLA_Q_RANK, MLA_Q_COLS)),
                  full((MLA_KV_RANK, 2 * nw)), full((1, nw)), full((1, MLA_PAIRS * LANES)),
                  full((1, nw)), row(LANES), row(LANES)],
        out_specs=[row(cat), row(cat), row(cat)],
        out_shape=[jax.ShapeDtypeStruct((TOKENS, cat), BF16)] * 3,
        compiler_params=_params(("arbitrary",)),
        name="mla_up",
    )(cqn, ckvn, kpe, wq_p, wkv_p, gqn, gqr, gkn, cos_t, sin_t)


def _mla_attn_body(q_ref, k_ref, v_ref, o_ref):
    tq, tk = MLA_TQ, MLA_TK
    qi = pl.program_id(2)
    lane = lax.broadcasted_iota(I32, (tq, MLA_CAT), 1)
    rot = lane - LANES
    head0 = (lane < 64) | ((rot >= 0) & (rot < 64) & ((rot % MLA_ROPE) < MLA_HALF))
    head1 = ((lane >= 64) & (lane < LANES)) | ((rot >= 0) & (rot < 64) & ((rot % MLA_ROPE) >= MLA_HALF))
    q = q_ref[...]
    zero = jnp.zeros_like(q)
    q2 = jnp.concatenate([jnp.where(head0, q, zero), jnp.where(head1, q, zero)], axis=0)

    def scores(ki):
        ks = pl.multiple_of(ki * tk, tk)
        return lax.dot_general(q2, k_ref[pl.ds(ks, tk), :], NT_DIMS, preferred_element_type=F32)

    def step(ki, s, carry, masked):
        ks = pl.multiple_of(ki * tk, tk)
        if masked:
            rowg = qi * tq + lax.broadcasted_iota(I32, (tq, tk), 0)
            colg = ki * tk + lax.broadcasted_iota(I32, (tq, tk), 1)
            keep = colg <= rowg
        out = []
        for h in range(2):
            m, acc = carry[h]
            sh = s[h * tq:(h + 1) * tq]
            if masked:
                sh = jnp.where(keep, sh, -jnp.inf)
            m_new = jnp.maximum(m, jnp.max(sh, axis=-1, keepdims=True))
            pb = jnp.exp2(sh - m_new).astype(BF16)
            pv = jnp.dot(pb, v_ref[pl.ds(ks, tk), h * LANES:(h + 1) * LANES], preferred_element_type=F32)
            out.append((m_new, jnp.exp2(m - m_new) * acc + pv))
        return tuple(out)

    init = tuple((jnp.full((tq, 1), -jnp.inf, F32), jnp.zeros((tq, LANES), F32)) for _ in range(2))
    kd = (qi * tq) // tk
    carry = lax.fori_loop(0, kd, lambda ki, c: step(ki, scores(ki), c, False), init)
    (_, acc0), (_, acc1) = step(kd, scores(kd), carry, True)
    half = LANES // 2
    out0 = acc0 / pltpu.roll(acc0, half, axis=1)
    out1 = acc1 / pltpu.roll(acc1, half, axis=1)
    o_ref[...] = jnp.where(lax.broadcasted_iota(I32, (tq, LANES), 1) < half, out0, out1).astype(BF16)


def _mla_attn(qcat, kcat, vext):
    tq = MLA_TQ
    nq = SEQ // tq
    return pl.pallas_call(
        _mla_attn_body,
        grid=(BATCH, MLA_PAIRS, nq),
        in_specs=[pl.BlockSpec((tq, MLA_CAT), lambda b, p, q: (b * nq + q, p)),
                  pl.BlockSpec((SEQ, MLA_CAT), lambda b, p, q: (b, p)),
                  pl.BlockSpec((SEQ, MLA_CAT), lambda b, p, q: (b, p))],
        out_specs=pl.BlockSpec((tq, LANES), lambda b, p, q: (b * nq + q, p)),
        out_shape=jax.ShapeDtypeStruct((TOKENS, MLA_HEADS * MLA_V), BF16),
        compiler_params=_params(("arbitrary", "arbitrary", "arbitrary")),
        name="mla_attn",
    )(qcat, kcat, vext)


def _dil_attn_body(q_ref, k_ref, v_ref, o_ref, os_ref, ls_ref, bias_ref):
    blk = DA_BLOCK
    lo = lax.broadcasted_iota(I32, (blk, LANES), 1) < 64
    rowi = lax.broadcasted_iota(I32, (2 * blk, 2 * blk), 0) % blk
    colj = lax.broadcasted_iota(I32, (2 * blk, 2 * blk), 1)
    rel = rowi - colj
    assert all(window // dil == blk for window, dil in DA_PATTERNS)
    for first, off in ((0, 0), (1, blk)):
        dist = off + rel
        bias_ref[first] = jnp.where((dist >= 0) & (dist <= blk), 0.0, -jnp.inf)

    for pi, (window, dil) in enumerate(DA_PATTERNS):
        n_back = window // dil
        nb = SEQ // dil // blk

        def block(t, carry, pi=pi, dil=dil, n_back=n_back, nb=nb):
            r = t // nb
            n = t % nb
            kb = jnp.maximum(n - 1, 0)
            qs = r + n * (blk * dil)
            ks = r + kb * (blk * dil)
            if dil == 1:
                qs = pl.multiple_of(qs, blk)
                ks = pl.multiple_of(ks, blk)
                q = q_ref[pl.ds(qs, blk), :]
                k2 = k_ref[pl.ds(ks, 2 * blk), :]
                v2 = v_ref[pl.ds(ks, 2 * blk), :]
            else:
                q = q_ref[pl.ds(qs, blk, stride=dil), :]
                k2 = k_ref[pl.ds(ks, 2 * blk, stride=dil), :]
                v2 = v_ref[pl.ds(ks, 2 * blk, stride=dil), :]
            zero = jnp.zeros_like(q)
            q2 = jnp.concatenate([jnp.where(lo, q, zero), jnp.where(lo, zero, q)], axis=0).astype(BF16)
            s = lax.dot_general(q2, k2.astype(BF16), NT_DIMS, preferred_element_type=F32)
            s = s + bias_ref[n - kb]
            m = jnp.max(s, axis=-1, keepdims=True)
            e = jnp.exp(s - m)
            l = jnp.sum(e, axis=-1, keepdims=True)
            pv = jnp.dot((e * (1.0 / l)).astype(BF16), v2.astype(BF16), preferred_element_type=F32)
            lse = m + jnp.log(l)
            o_blk = jnp.where(lo, pv[:blk], pv[blk:])
            l_blk = jnp.where(lo, jnp.broadcast_to(lse[:blk], (blk, LANES)),
                              jnp.broadcast_to(lse[blk:], (blk, LANES)))
            if dil == 1:
                os_ref[pi, pl.ds(qs, blk), :] = o_blk
                ls_ref[pi, pl.ds(qs, blk), :] = l_blk
            else:
                os_ref[pi, pl.ds(qs, blk, stride=dil), :] = o_blk
                ls_ref[pi, pl.ds(qs, blk, stride=dil), :] = l_blk
            return carry

        def group(g, carry, block=block):
            for u in range(DA_INTERLEAVE):
                block(g * DA_INTERLEAVE + u, carry)
            return carry

        lax.fori_loop(0, dil * nb // DA_INTERLEAVE, group, 0)

    chunk = 512

    def combine(c, carry):
        sl = pl.ds(pl.multiple_of(c * chunk, chunk), chunk)
        l0 = ls_ref[0, sl, :]
        l1 = ls_ref[1, sl, :]
        l2 = ls_ref[2, sl, :]
        mx = jnp.maximum(jnp.maximum(l0, l1), l2)
        e0 = jnp.exp(l0 - mx)
        e1 = jnp.exp(l1 - mx)
        e2 = jnp.exp(l2 - mx)
        den = e0 + e1 + e2
        o = (e0 * os_ref[0, sl, :] + e1 * os_ref[1, sl, :] + e2 * os_ref[2, sl, :]) / den
        o_ref[sl, :] = o.astype(BF16)
        return carry

    lax.fori_loop(0, SEQ // chunk, combine, 0)


def _dil_attn(qa, ka, va):
    pairs = DA_HEADS // 2
    spec = pl.BlockSpec((SEQ, LANES), lambda b, p: (b, p))
    n_pat = len(DA_PATTERNS)
    return pl.pallas_call(
        _dil_attn_body,
        grid=(BATCH, pairs),
        in_specs=[spec, spec, spec],
        out_specs=spec,
        out_shape=jax.ShapeDtypeStruct((TOKENS, DA_WIDTH), BF16),
        scratch_shapes=[pltpu.VMEM((n_pat, SEQ, LANES), F32), pltpu.VMEM((n_pat, SEQ, LANES), F32),
                        pltpu.VMEM((2, 2 * DA_BLOCK, 2 * DA_BLOCK), F32)],
        compiler_params=_params(("arbitrary", "arbitrary")),
        name="dil_attn",
    )(qa, ka, va)


def _out_proj_body(*refs, n_in, has_bias):
    a_refs = refs[:n_in]
    w_refs = refs[n_in:2 * n_in]
    rest = refs[2 * n_in:]
    if has_bias:
        b_ref, x_ref, g_ref, o_ref = rest
    else:
        x_ref, g_ref, o_ref = rest
    y = jnp.dot(a_refs[0][...], w_refs[0][...], preferred_element_type=F32)
    for a_ref, w_ref in zip(a_refs[1:], w_refs[1:]):
        y = y + jnp.dot(a_ref[...], w_ref[...], preferred_element_type=F32)
    if has_bias:
        y = y + b_ref[...]
    o_ref[...] = x_ref[...] + g_ref[0] * y


def _out_proj(acts, weights, bias, x, mods):
    tm = ROW_TILE
    tpb = SEQ // tm
    in_specs = [pl.BlockSpec((tm, a.shape[1]), lambda i: (i, 0)) for a in acts]
    in_specs += [pl.BlockSpec(w.shape, lambda i: (0, 0)) for w in weights]
    args = list(acts) + list(weights)
    if bias is not None:
        in_specs.append(pl.BlockSpec((1, D_MODEL), lambda i: (0, 0)))
        args.append(bias)
    in_specs += [pl.BlockSpec((tm, D_MODEL), lambda i: (i, 0)), _mod_spec(2, tpb)]
    args += [x, mods]
    return pl.pallas_call(
        functools.partial(_out_proj_body, n_in=len(acts), has_bias=bias is not None),
        grid=(TOKENS // tm,),
        in_specs=in_specs,
        out_specs=pl.BlockSpec((tm, D_MODEL), lambda i: (i, 0)),
        out_shape=jax.ShapeDtypeStruct((TOKENS, D_MODEL), F32),
        compiler_params=_params(("arbitrary",)),
        name="out_proj",
    )(*args)


def _in_rec_body(x_ref, sh_ref, sc_ref, g_ref, w_ref, b_ref, gate_ref, xr_ref):
    h = _prenorm(x_ref[...], g_ref[...], sh_ref[0], sc_ref[0])
    gx = jnp.dot(h.astype(BF16), w_ref[...], preferred_element_type=F32) + b_ref[...]
    gate_ref[...] = jax.nn.gelu(gx[:, :LRU_WIDTH], approximate=True)
    xr_ref[...] = gx[:, LRU_WIDTH:]


def _in_rec(x, mods, gain, w, b):
    tm = ROW_TILE
    tpb = SEQ // tm
    row = lambda cols: pl.BlockSpec((tm, cols), lambda i: (i, 0))
    full = lambda shape: pl.BlockSpec(shape, lambda i: (0,) * len(shape))
    return pl.pallas_call(
        _in_rec_body,
        grid=(TOKENS // tm,),
        in_specs=[row(D_MODEL), _mod_spec(0, tpb), _mod_spec(1, tpb), full((1, D_MODEL)),
                  full((D_MODEL, 2 * LRU_WIDTH)), full((1, 2 * LRU_WIDTH))],
        out_specs=[row(LRU_WIDTH), row(LRU_WIDTH)],
        out_shape=[jax.ShapeDtypeStruct((TOKENS, LRU_WIDTH), F32)] * 2,
        compiler_params=_params(("arbitrary",)),
        name="in_rec",
    )(x, mods, mods, gain, w, b)


def _rglru_body(xr_ref, g_ref, cw_ref, cb_ref, wa_ref, ba_ref, wx_ref, bx_ref, lam_ref, o_ref,
                tail_ref, h_ref):
    ts = SCAN_ROWS
    width = LRU_WIDTH

    @pl.when(pl.program_id(1) == 0)
    def _():
        tail_ref[...] = jnp.zeros_like(tail_ref)
        h_ref[...] = jnp.zeros_like(h_ref)

    xr = xr_ref[...]
    tail = tail_ref[...]
    row8 = lax.broadcasted_iota(I32, (SUBLANES, width), 0)
    cw = cw_ref[...]
    xc = xr * cw[CONV_WIDTH - 1:CONV_WIDTH, :] + cb_ref[...]
    for s in range(1, CONV_WIDTH):
        rolled = pltpu.roll(xr, s, axis=0)
        fix = pltpu.roll(tail, s, axis=0)
        first = jnp.where(row8 < s, fix, rolled[:SUBLANES])
        shifted = jnp.concatenate([first, rolled[SUBLANES:]], axis=0)
        xc = xc + shifted * cw[CONV_WIDTH - 1 - s:CONV_WIDTH - s, :]
    tail_ref[...] = xr[ts - SUBLANES:, :]

    xcb = xc.astype(BF16)
    bd = LRU_BLOCK_DIM
    ra = jnp.concatenate([jnp.dot(xcb[:, n * bd:(n + 1) * bd], wa_ref[n], preferred_element_type=F32)
                          for n in range(LRU_BLOCKS)], axis=1) + ba_ref[...]
    rx = jnp.concatenate([jnp.dot(xcb[:, n * bd:(n + 1) * bd], wx_ref[n], preferred_element_type=F32)
                          for n in range(LRU_BLOCKS)], axis=1) + bx_ref[...]
    r = jax.nn.sigmoid(ra)
    ig = jax.nn.sigmoid(rx)
    z = -lam_ref[...]
    softplus = jnp.maximum(z, 0.0) + jnp.log1p(jnp.exp(-jnp.abs(z)))
    log_a = (-LRU_C) * r * softplus
    a = jnp.exp(log_a)
    bt = jnp.sqrt(-jnp.tanh(log_a) * (a * a + 1.0)) * (ig * xc)

    rowi = lax.broadcasted_iota(I32, (ts, LANES), 0)
    hs = []
    for c in range(width // LANES):
        ac = a[:, c * LANES:(c + 1) * LANES]
        bc = bt[:, c * LANES:(c + 1) * LANES]
        k = 1
        while k < ts:
            keep = rowi >= k
            a_sh = jnp.where(keep, pltpu.roll(ac, k, axis=0), 1.0)
            b_sh = jnp.where(keep, pltpu.roll(bc, k, axis=0), 0.0)
            bc = ac * b_sh + bc
            ac = ac * a_sh
            k *= 2
        hs.append(ac * h_ref[0:1, c * LANES:(c + 1) * LANES] + bc)
    h = jnp.concatenate(hs, axis=1)
    h_ref[0:1, :] = h[ts - 1:ts, :]
    o_ref[...] = (g_ref[...] * h).astype(BF16)


def _rglru(xr, gate, cw, cb, wa, ba, wx, bx, lam):
    ts = SCAN_ROWS
    nt = SEQ // ts
    row = pl.BlockSpec((ts, LRU_WIDTH), lambda b, j: (b * nt + j, 0))
    full = lambda shape: pl.BlockSpec(shape, lambda b, j: (0,) * len(shape))
    vec = full((1, LRU_WIDTH))
    wspec = full((LRU_BLOCKS, LRU_BLOCK_DIM, LRU_BLOCK_DIM))
    return pl.pallas_call(
        _rglru_body,
        grid=(BATCH, nt),
        in_specs=[row, row, full((CONV_WIDTH, LRU_WIDTH)), vec, wspec, vec, wspec, vec, vec],
        out_specs=row,
        out_shape=jax.ShapeDtypeStruct((TOKENS, LRU_WIDTH), BF16),
        scratch_shapes=[pltpu.VMEM((SUBLANES, LRU_WIDTH), F32), pltpu.VMEM((SUBLANES, LRU_WIDTH), F32)],
        compiler_params=_params(("arbitrary", "arbitrary")),
        name="rglru",
    )(xr, gate, cw, cb, wa, ba, wx, bx, lam)


def _router_body(x_ref, sh_ref, sc_ref, g_ref, wr_ref, br_ref,
                 hn_ref, idx_ref, rank_ref, gate_ref, cnt_ref, carry_ref):
    tm = ROW_TILE

    @pl.when(pl.program_id(0) == 0)
    def _():
        carry_ref[...] = jnp.zeros_like(carry_ref)

    h = _prenorm(x_ref[...], g_ref[...], sh_ref[0], sc_ref[0])
    _store_row_tiles(hn_ref, h)
    hb = h.astype(BF16)
    hl = (h - hb.astype(F32)).astype(BF16)
    wr = wr_ref[...]
    wh = wr.astype(BF16)
    wl = (wr - wh.astype(F32)).astype(BF16)
    logits = (jnp.dot(hb, wh, preferred_element_type=F32) + jnp.dot(hb, wl, preferred_element_type=F32)
              + jnp.dot(hl, wh, preferred_element_type=F32)) + br_ref[...]
    lane = lax.broadcasted_iota(I32, (tm, LANES), 1)
    lanef = lane.astype(F32)
    neg = -jnp.inf
    work = logits
    tops, hots = [], []
    for _ in range(TOP_K):
        m = jnp.max(work, axis=-1, keepdims=True)
        first = jnp.min(jnp.where(work == m, lanef, float(LANES)), axis=-1, keepdims=True)
        hot = lanef == first
        work = jnp.where(hot, neg, work)
        tops.append((m, first))
        hots.append(hot)
    exps = [jnp.exp(m - tops[0][0]) for m, _ in tops]
    den = exps[0] + exps[1] + exps[2] + exps[3]
    sel = jnp.zeros((tm, LANES), F32)
    for hot in hots:
        sel = jnp.where(hot, 1.0, sel)
    ri = lax.broadcasted_iota(I32, (tm, tm), 0)
    ci = lax.broadcasted_iota(I32, (tm, tm), 1)
    tril = jnp.where(ci < ri, 1.0, 0.0).astype(BF16)
    cum = jnp.dot(tril, sel.astype(BF16), preferred_element_type=F32) + carry_ref[0:1, :]
    idx_slab = jnp.zeros((tm, LANES), I32)
    rank_slab = jnp.zeros((tm, LANES), I32)
    gate_slab = jnp.zeros((tm, LANES), F32)
    for k in range(TOP_K):
        rank = jnp.sum(jnp.where(hots[k], cum, 0.0), axis=-1, keepdims=True)
        idx_slab = jnp.where(lane == k, tops[k][1].astype(I32), idx_slab)
        rank_slab = jnp.where(lane == k, rank.astype(I32), rank_slab)
        gate_slab = jnp.where(lane == k, exps[k] / den, gate_slab)
    idx_ref[...] = idx_slab
    rank_ref[...] = rank_slab
    gate_ref[...] = gate_slab
    total = carry_ref[0:1, :] + jnp.sum(sel, axis=0, keepdims=True)
    carry_ref[0:1, :] = total
    cnt_ref[...] = total


def _router(x, mods, gain, wr_p, br_p):
    tm = ROW_TILE
    tpb = SEQ // tm
    row = lambda cols: pl.BlockSpec((tm, cols), lambda i: (i, 0))
    full = lambda shape: pl.BlockSpec(shape, lambda i: (0,) * len(shape))
    return pl.pallas_call(
        _router_body,
        grid=(TOKENS // tm,),
        in_specs=[row(D_MODEL), _mod_spec(3, tpb), _mod_spec(4, tpb), full((1, D_MODEL)),
                  full((D_MODEL, LANES)), full((1, LANES))],
        out_specs=[pl.BlockSpec((tm * ROW_CHUNKS, LANES), lambda i: (i, 0)),
                   row(LANES), row(LANES), row(LANES), full((1, LANES))],
        out_shape=[jax.ShapeDtypeStruct((TOKENS * ROW_CHUNKS, LANES), F32),
                   jax.ShapeDtypeStruct((TOKENS, LANES), I32),
                   jax.ShapeDtypeStruct((TOKENS, LANES), I32),
                   jax.ShapeDtypeStruct((TOKENS, LANES), F32),
                   jax.ShapeDtypeStruct((1, LANES), F32)],
        scratch_shapes=[pltpu.VMEM((SUBLANES, LANES), F32)],
        compiler_params=_params(("arbitrary",)),
        name="router",
    )(x, mods, mods, gain, wr_p, br_p)


def _tile_copy(src, src_row, dst, dst_row, sem):
    s0 = pl.multiple_of(src_row * ROW_CHUNKS, ROW_CHUNKS)
    d0 = pl.multiple_of(dst_row * ROW_CHUNKS, ROW_CHUNKS)
    return pltpu.make_async_copy(src.at[pl.ds(s0, ROW_CHUNKS), :], dst.at[pl.ds(d0, ROW_CHUNKS), :], sem)


def _store_row_tiles(ref, val):
    rows = val.shape[0]
    for c in range(ROW_CHUNKS):
        ref[pl.ds(c, rows, stride=ROW_CHUNKS), :] = val[:, c * LANES:(c + 1) * LANES]


def _load_row_tiles(ref, rows):
    return jnp.concatenate([ref[pl.ds(c, rows, stride=ROW_CHUNKS), :] for c in range(ROW_CHUNKS)], axis=1)


def _moe_dispatch_body(padbase_ref, npad_ref, nused_ref, dest_ref, hn_ref, xs_hbm, zero_ref, sem, *, n_blocks):
    td = DISPATCH_ROWS
    block_rows = MOE_ROWS * ROW_CHUNKS

    @pl.when(pl.program_id(0) == 0)
    def _():
        zero_ref[...] = jnp.zeros_like(zero_ref)

        def per_expert(e, carry, wait):
            n = npad_ref[e]
            size = MOE_ROWS // 2
            while size >= 1:
                @pl.when((n & size) != 0)
                def _(size=size):
                    first = padbase_ref[e] + (n & ~(2 * size - 1))
                    start = pl.multiple_of(first * ROW_CHUNKS, ROW_CHUNKS)
                    cp = pltpu.make_async_copy(zero_ref.at[pl.ds(0, size * ROW_CHUNKS), :],
                                               xs_hbm.at[pl.ds(start, size * ROW_CHUNKS), :], sem)
                    if wait:
                        cp.wait()
                    else:
                        cp.start()

                size //= 2
            return carry

        def per_block(b, carry, wait):
            start = pl.multiple_of(b * block_rows, block_rows)
            cp = pltpu.make_async_copy(zero_ref, xs_hbm.at[pl.ds(start, block_rows), :], sem)
            if wait:
                cp.wait()
            else:
                cp.start()
            return carry

        for wait in (False, True):
            lax.fori_loop(0, N_EXPERTS, functools.partial(per_expert, wait=wait), 0)
            lax.fori_loop(nused_ref[0], n_blocks, functools.partial(per_block, wait=wait), 0)

    def issue(r, carry):
        for k in range(TOP_K):
            _tile_copy(hn_ref, r, xs_hbm, dest_ref[0, 0, r * TOP_K + k], sem).start(priority=k % 2)
        return carry

    lax.fori_loop(0, td, issue, 0, unroll=8)

    def drain(r, carry):
        for k in range(TOP_K):
            _tile_copy(hn_ref, r, xs_hbm, 0, sem).wait()
        return carry

    lax.fori_loop(0, td, drain, 0, unroll=8)


def _moe_dispatch(pad_base, n_pad, n_used, dest, hn_tiles, n_blocks):
    td = DISPATCH_ROWS
    nt = TOKENS // td
    grid_spec = pltpu.PrefetchScalarGridSpec(
        num_scalar_prefetch=3,
        grid=(nt,),
        in_specs=[pl.BlockSpec((1, 1, td * TOP_K), lambda i, *_: (i, 0, 0), memory_space=pltpu.SMEM),
                  pl.BlockSpec((td * ROW_CHUNKS, LANES), lambda i, *_: (i, 0))],
        out_specs=pl.BlockSpec(memory_space=pl.ANY),
        scratch_shapes=[pltpu.VMEM((MOE_ROWS * ROW_CHUNKS, LANES), F32), pltpu.SemaphoreType.DMA],
    )
    return pl.pallas_call(
        functools.partial(_moe_dispatch_body, n_blocks=n_blocks),
        grid_spec=grid_spec,
        out_shape=jax.ShapeDtypeStruct((n_blocks * MOE_ROWS * ROW_CHUNKS, LANES), F32),
        compiler_params=_params(("arbitrary",)),
        name="moe_dispatch",
    )(pad_base, n_pad, n_used, dest.reshape(nt, 1, td * TOP_K), hn_tiles)


def _moe_ffn_body(be_ref, nused_ref, first_ref, slot_ref, nexte_ref, xs_ref, b1_ref, b2_ref, w1_hbm, w2_hbm,
                  ys_ref, w1_buf, w2_buf, ht_ref, sem, *, base):
    bm = MOE_ROWS
    i = pl.program_id(0)

    def weight_copies(expert, slot):
        return (pltpu.make_async_copy(w1_hbm.at[base + expert], w1_buf.at[slot], sem.at[0, slot]),
                pltpu.make_async_copy(w2_hbm.at[base + expert], w2_buf.at[slot], sem.at[1, slot]))

    @pl.when(i < nused_ref[0])
    def _():
        slot = slot_ref[i]

        @pl.when(i == 0)
        def _():
            for cp in weight_copies(be_ref[0], 0):
                cp.start()

        @pl.when(first_ref[i] == 1)
        def _():
            for cp in weight_copies(be_ref[i], slot):
                cp.wait()

            @pl.when(nexte_ref[i] < N_EXPERTS)
            def _():
                for cp in weight_copies(nexte_ref[i], 1 - slot):
                    cp.start()

        h = jnp.dot(_load_row_tiles(xs_ref, bm), w1_buf[slot], preferred_element_type=F32) + b1_ref[0]
        ht = h.T
        for j in range(bm // LANES):
            ht_ref[j] = ht[:, j * LANES:(j + 1) * LANES]
        parts = []
        for j in range(bm // LANES):
            gate = jnp.minimum(ht_ref[j, pl.ds(0, D_EXPERT, stride=2), :], SWIGLU_LIMIT)
            lin = jnp.clip(ht_ref[j, pl.ds(1, D_EXPERT, stride=2), :], -SWIGLU_LIMIT, SWIGLU_LIMIT)
            parts.append(gate * jax.nn.sigmoid(SWIGLU_ALPHA * gate) * (lin + 1.0))
        act_t = jnp.concatenate(parts, axis=1)
        y = lax.dot_general(act_t, w2_buf[slot], (((0,), (0,)), ((), ())), preferred_element_type=F32)
        _store_row_tiles(ys_ref, y + b2_ref[0])

    @pl.when(i >= nused_ref[0])
    def _():
        ys_ref[...] = jnp.zeros_like(ys_ref)


def _moe_ffn(block_expert, n_used, first, slot, next_expert, xs, w1, b1, w2, b2, layer, n_blocks):
    bm = MOE_ROWS
    f2 = 2 * D_EXPERT
    base = layer * N_EXPERTS
    grid_spec = pltpu.PrefetchScalarGridSpec(
        num_scalar_prefetch=5,
        grid=(n_blocks,),
        in_specs=[pl.BlockSpec((bm * ROW_CHUNKS, LANES), lambda i, be, nu, *_: (jnp.minimum(i, nu[0] - 1), 0)),
                  pl.BlockSpec((1, 1, f2), lambda i, be, *_: (base + be[i], 0, 0)),
                  pl.BlockSpec((1, 1, D_MODEL), lambda i, be, *_: (base + be[i], 0, 0)),
                  pl.BlockSpec(memory_space=pl.ANY), pl.BlockSpec(memory_space=pl.ANY)],
        out_specs=pl.BlockSpec((bm * ROW_CHUNKS, LANES), lambda i, *_: (i, 0)),
        scratch_shapes=[pltpu.VMEM((2, D_MODEL, f2), F32), pltpu.VMEM((2, D_EXPERT, D_MODEL), F32),
                        pltpu.VMEM((bm // LANES, f2, LANES), F32), pltpu.SemaphoreType.DMA((2, 2))],
    )
    return pl.pallas_call(
        functools.partial(_moe_ffn_body, base=base),
        grid_spec=grid_spec,
        out_shape=jax.ShapeDtypeStruct((n_blocks * bm * ROW_CHUNKS, LANES), F32),
        compiler_params=_params(("arbitrary",)),
        name="moe_ffn",
    )(block_expert, n_used, first, slot, next_expert, xs,
      b1.reshape(DEPTH * N_EXPERTS, 1, f2), b2.reshape(DEPTH * N_EXPERTS, 1, D_MODEL),
      w1.reshape(DEPTH * N_EXPERTS, D_MODEL, f2), w2.reshape(DEPTH * N_EXPERTS, D_EXPERT, D_MODEL))


def _moe_combine_body(dest_ref, dnext_ref, gates_ref, x_ref, g_ref, ys_hbm, o_ref, buf0_ref, buf1_ref, sem):
    tc = COMBINE_ROWS
    i = pl.program_id(0)
    bufs = (buf0_ref, buf1_ref)

    def issue(d_ref, slot):
        def body(r, carry):
            for k in range(TOP_K):
                _tile_copy(ys_hbm, d_ref[0, 0, r * TOP_K + k], bufs[slot].at[k], r,
                           sem.at[slot]).start(priority=k % 2)
            return carry

        lax.fori_loop(0, tc, body, 0, unroll=8)

    def drain(slot):
        def body(r, carry):
            for k in range(TOP_K):
                _tile_copy(ys_hbm, 0, bufs[slot].at[k], r, sem.at[slot]).wait()
            return carry

        lax.fori_loop(0, tc, body, 0, unroll=8)

    def finish(slot):
        drain(slot)
        gates = gates_ref[...]
        gate = g_ref[0]
        for c in range(ROW_CHUNKS):
            rows = pl.ds(c, tc, stride=ROW_CHUNKS)
            y = gates[:, 0:1] * bufs[slot][0, rows, :]
            for k in range(1, TOP_K):
                y = y + gates[:, k:k + 1] * bufs[slot][k, rows, :]
            cols = slice(c * LANES, (c + 1) * LANES)
            o_ref[:, cols] = x_ref[:, cols] + gate[:, cols] * y

    @pl.when(i == 0)
    def _():
        issue(dest_ref, 0)

    for slot in range(2):
        @pl.when(i % 2 == slot)
        def _(slot=slot):
            @pl.when(i + 1 < pl.num_programs(0))
            def _():
                issue(dnext_ref, 1 - slot)

            finish(slot)


def _moe_combine(dest, gates, x, mods, ys):
    tc = COMBINE_ROWS
    tpb = SEQ // tc
    nt = TOKENS // tc
    dest3 = dest.reshape(nt, 1, tc * TOP_K)
    buf = pltpu.VMEM((TOP_K, tc * ROW_CHUNKS, LANES), F32)
    return pl.pallas_call(
        _moe_combine_body,
        grid=(nt,),
        in_specs=[pl.BlockSpec((1, 1, tc * TOP_K), lambda i: (i, 0, 0), memory_space=pltpu.SMEM),
                  pl.BlockSpec((1, 1, tc * TOP_K), lambda i: (jnp.minimum(i + 1, nt - 1), 0, 0),
                               memory_space=pltpu.SMEM),
                  pl.BlockSpec((tc, LANES), lambda i: (i, 0)),
                  pl.BlockSpec((tc, D_MODEL), lambda i: (i, 0)),
                  _mod_spec(5, tpb),
                  pl.BlockSpec(memory_space=pl.ANY)],
        out_specs=pl.BlockSpec((tc, D_MODEL), lambda i: (i, 0)),
        out_shape=jax.ShapeDtypeStruct((TOKENS, D_MODEL), F32),
        scratch_shapes=[buf, buf, pltpu.SemaphoreType.DMA((2,))],
        compiler_params=_params(("arbitrary",)),
        name="moe_combine",
    )(dest3, dest3, gates, x, mods, ys)


def _attn_layer(x, mods, gain, j, cos_t, sin_t, p):
    w_in = p["attn_w_in"][j]
    rope_cols = A_COLS + MLA_Q_RANK + MLA_KV_RANK
    w_p = jnp.concatenate(
        [w_in[:, :rope_cols],
         jnp.tile(w_in[:, rope_cols:rope_cols + MLA_HALF], (1, MLA_HEADS)),
         jnp.tile(w_in[:, rope_cols + MLA_HALF:], (1, MLA_HEADS))], axis=1).astype(BF16)
    tile_h = lambda v: jnp.tile(v, MLA_HEADS).reshape(1, -1)
    kg = p["mla_k_gain"][j]
    qg = p["mla_q_gain"][j]
    qa, ka, va, cqn, ckvn, kpe = _in_attn(
        x, mods, gain, w_p, tile_h(p["da_q_gain"][j]), tile_h(p["da_k_gain"][j]),
        p["mla_qa_gain"][j].reshape(1, -1), p["mla_kva_gain"][j].reshape(1, -1),
        tile_h(kg[MLA_NOPE:MLA_NOPE + MLA_HALF]), tile_h(kg[MLA_NOPE + MLA_HALF:]), cos_t, sin_t)

    wq = p["mla_w_uq"][j].reshape(MLA_Q_RANK, MLA_HEADS, MLA_QK)
    first = slice(MLA_NOPE, MLA_NOPE + MLA_HALF)
    second = slice(MLA_NOPE + MLA_HALF, MLA_QK)
    pad_cols = jnp.zeros((MLA_Q_RANK, LANES - 2 * MLA_ROPE), F32)
    rot_blocks = [jnp.concatenate([wq[:, 2 * q, first], wq[:, 2 * q + 1, first],
                                   wq[:, 2 * q, second], wq[:, 2 * q + 1, second], pad_cols], axis=1)
                  for q in range(MLA_PAIRS)]
    wq_p = jnp.concatenate([wq[:, :, :MLA_NOPE].reshape(MLA_Q_RANK, -1)] + rot_blocks, axis=1).astype(BF16)
    gqr = jnp.tile(jnp.concatenate([qg[first], qg[first], qg[second], qg[second],
                                    jnp.zeros((LANES - 2 * MLA_ROPE,), F32)]), MLA_PAIRS).reshape(1, -1)
    wkv = p["mla_w_ukv"][j].reshape(MLA_KV_RANK, MLA_HEADS, MLA_NOPE + MLA_V)
    wkv_p = jnp.concatenate(
        [wkv[:, :, :MLA_NOPE].reshape(MLA_KV_RANK, -1),
         wkv[:, :, MLA_NOPE:].reshape(MLA_KV_RANK, -1)], axis=1).astype(BF16)
    qcat, kcat, vext = _mla_up(cqn, ckvn, kpe, wq_p, wkv_p, tile_h(qg[:MLA_NOPE]), gqr,
                               tile_h(kg[:MLA_NOPE]), cos_t, sin_t)

    o_a = _dil_attn(qa, ka, va)
    o_b = _mla_attn(qcat, kcat, vext)
    w_out = p["attn_w_out"][j].astype(BF16)
    return _out_proj([o_a, o_b], [w_out[:DA_WIDTH], w_out[DA_WIDTH:]], None, x, mods)


def _rec_layer(x, mods, gain, j, p):
    gate, xr = _in_rec(x, mods, gain, p["rec_w_in"][j].astype(BF16), p["rec_b_in"][j].reshape(1, -1))
    gy = _rglru(xr, gate, p["rec_conv_w"][j], p["rec_conv_b"][j].reshape(1, -1),
                p["rec_wa"][j].astype(BF16), p["rec_ba"][j].reshape(1, -1),
                p["rec_wx"][j].astype(BF16), p["rec_bx"][j].reshape(1, -1),
                p["rec_lambda"][j].reshape(1, -1))
    return _out_proj([gy], [p["rec_w_out"][j].astype(BF16)], p["rec_b_out"][j].reshape(1, -1), x, mods)


def _moe_layer(x, mods, gain, layer, p):
    bm = MOE_ROWS
    wr_p = jnp.pad(p["moe_w_router"][layer], ((0, 0), (0, LANES - N_EXPERTS)))
    br_p = jnp.pad(p["moe_b_router"][layer], (0, LANES - N_EXPERTS), constant_values=-jnp.inf).reshape(1, LANES)
    hn, idx_slab, rank_slab, gate_slab, cnt = _router(x, mods, gain, wr_p, br_p)

    counts = cnt[0, :N_EXPERTS].astype(I32)
    padded = (counts + bm - 1) // bm * bm
    pad_ends = jnp.cumsum(padded)
    pad_starts = pad_ends - padded
    n_blocks = TOKENS * TOP_K // bm + N_EXPERTS
    experts = jnp.arange(N_EXPERTS, dtype=I32)
    idx = idx_slab[:, :TOP_K]
    dest = rank_slab[:, :TOP_K] + jnp.sum(
        jnp.where(idx[:, :, None] == experts, pad_starts, 0), axis=-1, dtype=I32)
    block_start = jnp.arange(n_blocks, dtype=I32) * bm
    block_expert = jnp.minimum(
        jnp.sum(pad_ends[None, :] <= block_start[:, None], axis=-1, dtype=I32), N_EXPERTS - 1)
    n_used = (pad_ends[-1:] // bm).astype(I32)
    has_rows = padded > 0
    later = (experts[None, :] > experts[:, None]) & has_rows[None, :]
    next_of = jnp.min(jnp.where(later, experts[None, :], N_EXPERTS), axis=-1).astype(I32)
    slot_of = ((jnp.cumsum(has_rows.astype(I32)) - has_rows.astype(I32)) % 2).astype(I32)
    mine = block_expert[:, None] == experts
    pick = lambda table: jnp.sum(jnp.where(mine, table, 0), axis=-1, dtype=I32)
    first = (pick(pad_starts) == block_start).astype(I32)
    slot = pick(slot_of)
    next_expert = pick(next_of)

    xs = _moe_dispatch(pad_starts + counts, padded - counts, n_used, dest, hn, n_blocks)
    ys = _moe_ffn(block_expert, n_used, first, slot, next_expert, xs, p["moe_w1"], p["moe_b1"], p["moe_w2"], p["moe_b2"], layer, n_blocks)
    return _moe_combine(dest, gate_slab, x, mods, ys)


def kernel(x, c, positions, ada_w, ada_b, norm1_gain, norm2_gain, attn_w_in, da_q_gain, da_k_gain, mla_qa_gain, mla_kva_gain, mla_w_uq, mla_w_ukv, mla_q_gain, mla_k_gain, attn_w_out, rec_w_in, rec_b_in, rec_conv_w, rec_conv_b, rec_wa, rec_ba, rec_wx, rec_bx, rec_lambda, rec_w_out, rec_b_out, moe_w_router, moe_b_router, moe_w1, moe_b1, moe_w2, moe_b2):
    p = dict(attn_w_in=attn_w_in, da_q_gain=da_q_gain, da_k_gain=da_k_gain, mla_qa_gain=mla_qa_gain,
             mla_kva_gain=mla_kva_gain, mla_w_uq=mla_w_uq, mla_w_ukv=mla_w_ukv, mla_q_gain=mla_q_gain,
             mla_k_gain=mla_k_gain, attn_w_out=attn_w_out, rec_w_in=rec_w_in, rec_b_in=rec_b_in,
             rec_conv_w=rec_conv_w, rec_conv_b=rec_conv_b, rec_wa=rec_wa, rec_ba=rec_ba, rec_wx=rec_wx,
             rec_bx=rec_bx, rec_lambda=rec_lambda, rec_w_out=rec_w_out, rec_b_out=rec_b_out,
             moe_w_router=moe_w_router, moe_b_router=moe_b_router, moe_w1=moe_w1, moe_b1=moe_b1,
             moe_w2=moe_w2, moe_b2=moe_b2)
    mods_all = _ada(c, ada_w, ada_b)
    cos_t, sin_t = _rope_tables(positions)
    xt = x.reshape(TOKENS, D_MODEL)
    for layer in range(DEPTH):
        mods = mods_all[layer]
        j = layer // 2
        gain1 = norm1_gain[layer].reshape(1, D_MODEL)
        if layer % 2 == 0:
            xt = _attn_layer(xt, mods, gain1, j, cos_t, sin_t, p)
        else:
            xt = _rec_layer(xt, mods, gain1, j, p)
        xt = _moe_layer(xt, mods, norm2_gain[layer].reshape(1, D_MODEL), layer, p)
    return xt.reshape(BATCH, SEQ, D_MODEL)
```
